```python
import math
import jax, jax.numpy as jnp
from jax import lax
import numpy as np

D_MODEL = 1024
BATCH = 4
SEQ = 8192
DEPTH = 2

HEAD_DIM = 64
GRID_W = 64
BLOCK = 128
NA_HEADS = 8
NA_ROW_K = 8
NA_COL_K = 16
SW_Q_HEADS = 8
SW_KV_HEADS = 2
SW_WINDOW = 128
DIFF_HEADS = 4
N_BRANCH = 3
BRANCH_W = 512
D_FF = 2816
ROPE_THETA = 10000.0
EPS = 1e-6
NEG = -1e30

A_W = NA_HEADS * HEAD_DIM
B_QW = SW_Q_HEADS * HEAD_DIM
B_KVW = SW_KV_HEADS * HEAD_DIM
C_QKW = 2 * DIFF_HEADS * HEAD_DIM
C_VW = DIFF_HEADS * 2 * HEAD_DIM
SPLIT_SIZES = (A_W, A_W, A_W, B_QW, B_KVW, B_KVW, C_QKW, C_QKW, C_VW, N_BRANCH * D_MODEL)
IN_COLS = sum(SPLIT_SIZES)

kernel_name = "hybrid_natten_swa_diffattn_macaron_encoder"


def rmsnorm(x, g):
    xf = x.astype(jnp.float32)
    y = xf * lax.rsqrt(jnp.mean(xf * xf, axis=-1, keepdims=True) + EPS)
    return (y * g.astype(jnp.float32)).astype(x.dtype)


def swiglu(x, w_gu, w_down):
    gate, up = jnp.split(x @ w_gu, 2, axis=-1)
    return (jax.nn.silu(gate) * up) @ w_down


def rope_tables(s):
    pos = jnp.arange(s, dtype=jnp.float32)
    inv = ROPE_THETA ** (-jnp.arange(0, HEAD_DIM, 2, dtype=jnp.float32) / HEAD_DIM)
    ang = pos[:, None] * inv[None, :]
    ang = jnp.concatenate([ang, ang], axis=-1)
    return jnp.cos(ang), jnp.sin(ang)


def apply_rope(t, cos, sin):
    t1, t2 = jnp.split(t, 2, axis=-1)
    rot = jnp.concatenate([-t2, t1], axis=-1)
    return t * cos.astype(t.dtype) + rot * sin.astype(t.dtype)


def split_heads(t, n):
    b, s, _ = t.shape
    return t.reshape(b, s, n, HEAD_DIM).transpose(0, 2, 1, 3)


def neighbourhood_attention(q, k, v, rpb):
    b, h, s, d = q.shape
    rows = s // GRID_W
    kr = min(NA_ROW_K, rows)
    kc = NA_COL_K
    r = jnp.arange(rows)
    c = jnp.arange(GRID_W)
    row_start = jnp.clip(r - kr // 2, 0, rows - kr)
    col_start = jnp.clip(c - kc // 2, 0, GRID_W - kc)
    key_rows = row_start[:, None] + jnp.arange(kr)[None, :]
    qg = q.reshape(b, h, rows, GRID_W, d)
    kg = k.reshape(b, h, rows, GRID_W, d)[:, :, key_rows]
    vg = v.reshape(b, h, rows, GRID_W, d)[:, :, key_rows]
    scores = jnp.einsum('bhrcd,bhrijd->bhrcij', qg, kg).astype(jnp.float32) * (d ** -0.5)
    dr = key_rows - r[:, None] + (NA_ROW_K - 1)
    dc = jnp.clip(c[None, :] - c[:, None] + (NA_COL_K - 1), 0, 2 * NA_COL_K - 2)
    bias = rpb[:, dr[:, None, :, None], dc[None, :, None, :]]
    col_ok = (c[None, :] >= col_start[:, None]) & (c[None, :] < col_start[:, None] + kc)
    scores = scores + bias[None].astype(jnp.float32)
    scores = jnp.where(col_ok[None, None, None, :, None, :], scores, NEG)
    p = jax.nn.softmax(scores.reshape(b, h, rows, GRID_W, kr * GRID_W), axis=-1)
    p = p.reshape(b, h, rows, GRID_W, kr, GRID_W).astype(v.dtype)
    out = jnp.einsum('bhrcij,bhrijd->bhrcd', p, vg)
    return out.reshape(b, h, s, d)


def sliding_window_attention(q, k, v, sink):
    b, hq, s, d = q.shape
    hkv = k.shape[1]
    g = hq // hkv
    nb = s // BLOCK
    qb = q.reshape(b, hkv, g, nb, BLOCK, d)

    def band(t):
        tp = jnp.pad(t, ((0, 0), (0, 0), (BLOCK, BLOCK), (0, 0))).reshape(b, hkv, nb + 2, BLOCK, d)
        return jnp.concatenate([tp[:, :, :-2], tp[:, :, 1:-1], tp[:, :, 2:]], axis=3)

    kb, vb = band(k), band(v)
    scores = jnp.einsum('bkgnqd,bknjd->bkgnqj', qb, kb).astype(jnp.float32) * (d ** -0.5)
    qpos = jnp.arange(nb)[:, None] * BLOCK + jnp.arange(BLOCK)[None, :]
    kpos = (jnp.arange(nb)[:, None] - 1) * BLOCK + jnp.arange(3 * BLOCK)[None, :]
    ok = ((jnp.abs(qpos[:, :, None] - kpos[:, None, :]) <= SW_WINDOW)
          & (kpos[:, None, :] >= 0) & (kpos[:, None, :] < s))
    scores = jnp.where(ok, scores, NEG)
    sink_logit = jnp.broadcast_to(sink.astype(jnp.float32).reshape(1, hkv, g, 1, 1, 1),
                                  scores.shape[:-1] + (1,))
    p = jax.nn.softmax(jnp.concatenate([scores, sink_logit], axis=-1), axis=-1)[..., :-1]
    out = jnp.einsum('bkgnqj,bknjd->bkgnqd', p.astype(v.dtype), vb)
    return out.reshape(b, hq, s, d)


def differential_attention(q, k, v, lam, lam_init, subln_g):
    b, h, _, s, d = q.shape
    nb = s // BLOCK
    qb = jnp.moveaxis(q.reshape(b, h, 2, nb, BLOCK, d), 3, 0)

    def one_block(qblk):
        sc = jnp.einsum('bhtqd,bhtkd->bhtqk', qblk, k).astype(jnp.float32) * (d ** -0.5)
        p = jax.nn.softmax(sc, axis=-1)
        a = p[:, :, 0] - lam * p[:, :, 1]
        return jnp.einsum('bhqk,bhke->bhqe', a.astype(v.dtype), v)

    o = lax.map(one_block, qb)
    o = jnp.moveaxis(o, 0, 2).reshape(b, h, s, 2 * d)
    return rmsnorm(o, subln_g) * (1.0 - lam_init)


def setup_inputs(seed: int = 0) -> dict:
    key = jax.random.key(seed)
    ks = jax.random.split(key, 24)
    L, D, F = DEPTH, D_MODEL, D_FF

    def nrm(k, shape, scale):
        return jax.random.normal(k, shape, jnp.float32) * scale

    def gain(k, n):
        return 1.0 + nrm(k, (L, n), 0.05)

    return {
        "x": nrm(ks[0], (BATCH, SEQ, D), 1.0),
        "ffn1_pre_g": gain(ks[1], D),
        "ffn1_w_gu": nrm(ks[2], (L, D, 2 * F), D ** -0.5),
        "ffn1_w_down": nrm(ks[3], (L, F, D), F ** -0.5),
        "ffn1_post_g": gain(ks[4], D),
        "mix_pre_g": gain(ks[5], D),
        "w_in": nrm(ks[6], (L, D, IN_COLS), D ** -0.5),
        "na_rpb": nrm(ks[7], (L, NA_HEADS, 2 * NA_ROW_K - 1, 2 * NA_COL_K - 1), 0.5),
        "sw_sink": nrm(ks[8], (L, SW_Q_HEADS), 1.0),
        "diff_lambda_q1": nrm(ks[9], (L, HEAD_DIM), 0.1),
        "diff_lambda_k1": nrm(ks[10], (L, HEAD_DIM), 0.1),
        "diff_lambda_q2": nrm(ks[11], (L, HEAD_DIM), 0.1),
        "diff_lambda_k2": nrm(ks[12], (L, HEAD_DIM), 0.1),
        "diff_subln_g": gain(ks[13], 2 * HEAD_DIM),
        "w_branch": nrm(ks[14], (L, N_BRANCH, BRANCH_W, D), BRANCH_W ** -0.5),
        "b_gate": nrm(ks[15], (L, N_BRANCH, D), 0.1),
        "w_out": nrm(ks[16], (L, D, D), D ** -0.5),
        "mix_post_g": gain(ks[17], D),
        "ffn2_pre_g": gain(ks[18], D),
        "ffn2_w_gu": nrm(ks[19], (L, D, 2 * F), D ** -0.5),
        "ffn2_w_down": nrm(ks[20], (L, F, D), F ** -0.5),
        "ffn2_post_g": gain(ks[21], D),
    }


def reference(x, ffn1_pre_g, ffn1_w_gu, ffn1_w_down, ffn1_post_g, mix_pre_g, w_in, na_rpb,
              sw_sink, diff_lambda_q1, diff_lambda_k1, diff_lambda_q2, diff_lambda_k2,
              diff_subln_g, w_branch, b_gate, w_out, mix_post_g, ffn2_pre_g, ffn2_w_gu,
              ffn2_w_down, ffn2_post_g):
    b, s, _ = x.shape
    cos, sin = rope_tables(s)
    split_at = np.cumsum(SPLIT_SIZES)[:-1].tolist()
    h = x
    for l in range(DEPTH):
        y = swiglu(rmsnorm(h, ffn1_pre_g[l]), ffn1_w_gu[l], ffn1_w_down[l])
        h = h + 0.5 * rmsnorm(y, ffn1_post_g[l])

        u = rmsnorm(h, mix_pre_g[l])
        proj = u @ w_in[l]
        qa, ka, va, qb, kb, vb, qc, kc, vc, gl = jnp.split(proj, split_at, axis=-1)

        oa = neighbourhood_attention(split_heads(qa, NA_HEADS), split_heads(ka, NA_HEADS),
                                     split_heads(va, NA_HEADS), na_rpb[l])
        oa = oa.transpose(0, 2, 1, 3).reshape(b, s, A_W)

        ob = sliding_window_attention(apply_rope(split_heads(qb, SW_Q_HEADS), cos, sin),
                                      apply_rope(split_heads(kb, SW_KV_HEADS), cos, sin),
                                      split_heads(vb, SW_KV_HEADS), sw_sink[l])
        ob = ob.transpose(0, 2, 1, 3).reshape(b, s, B_QW)

        qc = apply_rope(qc.reshape(b, s, DIFF_HEADS, 2, HEAD_DIM).transpose(0, 2, 3, 1, 4), cos, sin)
        kc = apply_rope(kc.reshape(b, s, DIFF_HEADS, 2, HEAD_DIM).transpose(0, 2, 3, 1, 4), cos, sin)
        vc = vc.reshape(b, s, DIFF_HEADS, 2 * HEAD_DIM).transpose(0, 2, 1, 3)
        lam_init = 0.8 - 0.6 * math.exp(-0.3 * l)
        lam = (jnp.exp(jnp.sum(diff_lambda_q1[l].astype(jnp.float32) * diff_lambda_k1[l].astype(jnp.float32)))
               - jnp.exp(jnp.sum(diff_lambda_q2[l].astype(jnp.float32) * diff_lambda_k2[l].astype(jnp.float32)))
               + lam_init)
        oc = differential_attention(qc, kc, vc, lam, lam_init, diff_subln_g[l])
        oc = oc.transpose(0, 2, 1, 3).reshape(b, s, C_VW)

        branches = jnp.stack([oa, ob, oc], axis=2)
        yb = jnp.einsum('bsnc,ncd->bsnd', branches, w_branch[l])
        gates = jax.nn.sigmoid(gl.reshape(b, s, N_BRANCH, D_MODEL) + b_gate[l])
        merged = jnp.sum(gates * yb, axis=2)
        h = h + rmsnorm(merged @ w_out[l], mix_post_g[l])

        y = swiglu(rmsnorm(h, ffn2_pre_g[l]), ffn2_w_gu[l], ffn2_w_down[l])
        h = h + 0.5 * rmsnorm(y, ffn2_post_g[l])
    return h
```

```python
import functools
import math

import jax
import jax.numpy as jnp
from jax import lax
from jax.experimental import pallas as pl
from jax.experimental.pallas import tpu as pltpu

F32 = jnp.float32
BF16 = jnp.bfloat16

HEAD_DIM = 64
GRID_W = 64
NA_HEADS = 8
NA_ROW_K = 8
NA_COL_K = 16
SW_Q_HEADS = 8
SW_KV_HEADS = 2
SW_WINDOW = 128
DIFF_HEADS = 4
N_BRANCH = 3
ROPE_THETA = 10000.0
EPS = 1e-6
NEG = -1e30
LANES = 128
MIB = 1024 * 1024

A_W = NA_HEADS * HEAD_DIM
B_QW = SW_Q_HEADS * HEAD_DIM
B_KVW = SW_KV_HEADS * HEAD_DIM
C_W = 2 * DIFF_HEADS * HEAD_DIM
QKV_COLS = 3 * A_W + B_QW + 2 * B_KVW + 3 * C_W

TOKEN_TILE = 512
NA_ROWS_PER_STEP = 8
SW_Q_TILE = 256
DIFF_Q_TILE = 256
DIFF_KV_TILE = 512


def _rmsnorm(x, g):
    return x * lax.rsqrt(jnp.mean(x * x, axis=-1, keepdims=True) + EPS) * g


def _dot(a, b):
    return jnp.dot(a, b, preferred_element_type=F32)


def _dot_nt(a, b):
    return lax.dot_general(a, b, (((1,), (1,)), ((), ())), preferred_element_type=F32)


def _resident(shape):
    return pl.BlockSpec(shape, lambda *_: (0,) * len(shape), pipeline_mode=pl.Buffered(1))


def _low_half_mask():
    return lax.broadcasted_iota(jnp.int32, (1, LANES), 1) < HEAD_DIM


def _ffn_kernel(x_ref, pre_g_ref, wgu_ref, wdown_ref, post_g_ref, o_ref, *, f_chunks, d_ff):
    x = x_ref[...]
    xn = _rmsnorm(x, pre_g_ref[...]).astype(BF16)
    acc = None
    for c0, c1 in f_chunks:
        gate = _dot(xn, wgu_ref[:, c0:c1])
        up = _dot(xn, wgu_ref[:, d_ff + c0:d_ff + c1])
        act = (gate * jax.nn.sigmoid(gate) * up).astype(BF16)
        part = _dot(act, wdown_ref[c0:c1, :])
        acc = part if acc is None else acc + part
    o_ref[...] = x + 0.5 * _rmsnorm(acc, post_g_ref[...])


def _ffn(h, pre_g, w_gu, w_down, post_g):
    t, d = h.shape
    d_ff = w_down.shape[0]
    step = 1024
    f_chunks = tuple((c, min(c + step, d_ff)) for c in range(0, d_ff, step))
    row = pl.BlockSpec((TOKEN_TILE, d), lambda i: (i, 0))
    return pl.pallas_call(
        functools.partial(_ffn_kernel, f_chunks=f_chunks, d_ff=d_ff),
        grid=(t // TOKEN_TILE,),
        in_specs=[row, _resident((1, d)), _resident(w_gu.shape), _resident(w_down.shape),
                  _resident((1, d))],
        out_specs=row,
        out_shape=jax.ShapeDtypeStruct((t, d), F32),
        compiler_params=pltpu.CompilerParams(
            dimension_semantics=("arbitrary",), vmem_limit_bytes=48 * MIB),
        name="ffn",
    )(h, pre_g, w_gu, w_down, post_g)


def _rope(t, cos, sin_lo, sin_hi):
    out = []
    for c in range(t.shape[1] // LANES):
        tc = t[:, c * LANES:(c + 1) * LANES]
        fwd = pltpu.roll(tc, LANES - HEAD_DIM // 2, axis=1)
        back = pltpu.roll(tc, HEAD_DIM // 2, axis=1)
        out.append(tc * cos + fwd * sin_lo + back * sin_hi)
    return jnp.concatenate(out, axis=1)


def _proj_kernel(h_ref, g_ref, w_ref, cos_ref, sin_lo_ref, sin_hi_ref,
                 u_ref, qa_ref, ka_ref, va_ref, qb_ref, kb_ref, vb_ref, qc_ref, kc_ref, vc_ref):
    un = _rmsnorm(h_ref[...], g_ref[...]).astype(BF16)
    u_ref[...] = un
    cos, sin_lo, sin_hi = cos_ref[...], sin_lo_ref[...], sin_hi_ref[...]
    scale = HEAD_DIM ** -0.5
    col = 0

    def seg(width):
        nonlocal col
        out = _dot(un, w_ref[:, col:col + width])
        col += width
        return out

    qa_ref[...] = (seg(A_W) * scale).astype(BF16)
    ka_ref[...] = seg(A_W).astype(BF16)
    va_ref[...] = seg(A_W).astype(BF16)
    qb_ref[...] = (_rope(seg(B_QW), cos, sin_lo, sin_hi) * scale).astype(BF16)
    kb_ref[...] = _rope(seg(B_KVW), cos, sin_lo, sin_hi).astype(BF16)
    vb_ref[...] = seg(B_KVW).astype(BF16)
    qc_ref[...] = (_rope(seg(C_W), cos, sin_lo, sin_hi) * scale).astype(BF16)
    kc_ref[...] = _rope(seg(C_W), cos, sin_lo, sin_hi).astype(BF16)
    vc_ref[...] = seg(C_W).astype(BF16)


def _proj(h, g, w_qkv, cos, sin_lo, sin_hi):
    t, d = h.shape
    s = cos.shape[0]
    tiles_per_seq = s // TOKEN_TILE
    widths = (d, A_W, A_W, A_W, B_QW, B_KVW, B_KVW, C_W, C_W, C_W)
    rows = lambda w: pl.BlockSpec((TOKEN_TILE, w), lambda i: (i, 0))
    table = pl.BlockSpec((TOKEN_TILE, LANES), lambda i: (i % tiles_per_seq, 0))
    return pl.pallas_call(
        _proj_kernel,
        grid=(t // TOKEN_TILE,),
        in_specs=[rows(d), _resident((1, d)), _resident(w_qkv.shape), table, table, table],
        out_specs=[rows(w) for w in widths],
        out_shape=[jax.ShapeDtypeStruct((t, w), BF16) for w in widths],
        compiler_params=pltpu.CompilerParams(
            dimension_semantics=("arbitrary",), vmem_limit_bytes=40 * MIB),
        name="proj",
    )(h, g, w_qkv, cos, sin_lo, sin_hi)


def _na_kernel(q_ref, kp_ref, kc_ref, kn_ref, vp_ref, vc_ref, vn_ref, bias_ref, o_ref,
               kwin_ref, vwin_ref, *, grid_rows):
    j = pl.program_id(1)
    half = NA_ROWS_PER_STEP // 2 * GRID_W
    cur = NA_ROWS_PER_STEP * GRID_W
    kwin_ref[0:half, :] = kp_ref[...]
    kwin_ref[half:half + cur, :] = kc_ref[...]
    kwin_ref[half + cur:, :] = kn_ref[...]
    vwin_ref[0:half, :] = vp_ref[...]
    vwin_ref[half:half + cur, :] = vc_ref[...]
    vwin_ref[half + cur:, :] = vn_ref[...]
    low = _low_half_mask()
    keys = NA_ROW_K * GRID_W

    def one_row(i, carry):
        r = j * NA_ROWS_PER_STEP + i
        row_start = jnp.clip(r - NA_ROW_K // 2, 0, grid_rows - NA_ROW_K)
        off = pl.multiple_of((row_start - (j * NA_ROWS_PER_STEP - NA_ROWS_PER_STEP // 2)) * GRID_W,
                             GRID_W)
        dr0 = row_start - r + (NA_ROW_K - 1)
        q_row = q_ref[pl.ds(pl.multiple_of(i * GRID_W, GRID_W), GRID_W), :]
        chunks = []
        for c in range(A_W // LANES):
            k = kwin_ref[pl.ds(off, keys), c * LANES:(c + 1) * LANES]
            v = vwin_ref[pl.ds(off, keys), c * LANES:(c + 1) * LANES]
            qc = q_row[:, c * LANES:(c + 1) * LANES]
            res = []
            for hh in range(2):
                head = 2 * c + hh
                qz = jnp.where(low if hh == 0 else ~low, qc, jnp.zeros_like(qc))
                s = _dot_nt(qz, k)
                bias = jnp.concatenate(
                    [bias_ref[head, dr0 + 2 * t] for t in range(NA_ROW_K // 2)], axis=1)
                s = s + bias
                m = jnp.max(s, axis=-1, keepdims=True)
                p = jnp.exp(s - m)
                l = jnp.sum(p, axis=-1, keepdims=True)
                res.append(_dot(p.astype(BF16), v) / l)
            chunks.append(jnp.where(low, res[0], res[1]))
        o_ref[pl.ds(pl.multiple_of(i * GRID_W, GRID_W), GRID_W), :] = (
            jnp.concatenate(chunks, axis=1).astype(o_ref.dtype))
        return carry

    lax.fori_loop(0, NA_ROWS_PER_STEP, one_row, 0)


def _na_bias_table(rpb):
    c = jnp.arange(GRID_W)
    dc = jnp.clip(c[None, :] - c[:, None] + (NA_COL_K - 1), 0, 2 * NA_COL_K - 2)
    col_start = jnp.clip(c - NA_COL_K // 2, 0, GRID_W - NA_COL_K)
    col_ok = (c[None, :] >= col_start[:, None]) & (c[None, :] < col_start[:, None] + NA_COL_K)
    b = jnp.where(col_ok[None, None], rpb.astype(F32)[:, :, dc], NEG)
    return jnp.concatenate([b[:, :-1], b[:, 1:]], axis=-1)


def _na(q, k, v, bias):
    b, s, w = q.shape
    grid_rows = s // GRID_W
    cur = NA_ROWS_PER_STEP * GRID_W
    half = cur // 2
    n_half = s // half
    cur_spec = pl.BlockSpec((None, cur, w), lambda bi, j: (bi, j, 0))
    prev_spec = pl.BlockSpec((None, half, w), lambda bi, j: (bi, jnp.maximum(2 * j - 1, 0), 0))
    next_spec = pl.BlockSpec((None, half, w),
                             lambda bi, j: (bi, jnp.minimum(2 * j + 2, n_half - 1), 0))
    return pl.pallas_call(
        functools.partial(_na_kernel, grid_rows=grid_rows),
        grid=(b, s // cur),
        in_specs=[cur_spec, prev_spec, cur_spec, next_spec, prev_spec, cur_spec, next_spec,
                  _resident(bias.shape)],
        out_specs=cur_spec,
        out_shape=jax.ShapeDtypeStruct((b, s, w), BF16),
        scratch_shapes=[pltpu.VMEM((2 * cur, w), BF16), pltpu.VMEM((2 * cur, w), BF16)],
        compiler_params=pltpu.CompilerParams(
            dimension_semantics=("arbitrary", "arbitrary"), vmem_limit_bytes=32 * MIB),
        name="neighbourhood_attn",
    )(q, k, k, k, v, v, v, bias)


def _sw_kernel(sink_ref, q_ref, kp_ref, kc_ref, kn_ref, vp_ref, vc_ref, vn_ref, o_ref, *, seq):
    i = pl.program_id(1)
    tq = q_ref.shape[0]
    k = jnp.concatenate([kp_ref[...], kc_ref[...], kn_ref[...]], axis=0)
    v = jnp.concatenate([vp_ref[...], vc_ref[...], vn_ref[...]], axis=0)
    nk = k.shape[0]
    qpos = i * tq + lax.broadcasted_iota(jnp.int32, (tq, nk), 0)
    kpos = i * tq - SW_WINDOW + lax.broadcasted_iota(jnp.int32, (tq, nk), 1)
    ok = (jnp.abs(qpos - kpos) <= SW_WINDOW) & (kpos >= 0) & (kpos < seq)
    lane = lax.broadcasted_iota(jnp.int32, (1, LANES), 1)
    low = lane < HEAD_DIM
    group = SW_Q_HEADS // SW_KV_HEADS
    q = q_ref[...]
    for c in range(B_QW // LANES):
        qc = q[:, c * LANES:(c + 1) * LANES]
        qc_swapped = pltpu.roll(qc, HEAD_DIM, axis=1)
        res = []
        for hh in range(2):
            head = 2 * c + hh
            kv_head = head // group
            src = qc if hh == kv_head else qc_swapped
            qz = jnp.where(low if kv_head == 0 else ~low, src, jnp.zeros_like(src))
            s = jnp.where(ok, _dot_nt(qz, k), NEG)
            sink = sink_ref[0, head]
            m = jnp.maximum(jnp.max(s, axis=-1, keepdims=True), sink)
            p = jnp.exp(s - m)
            l = jnp.sum(p, axis=-1, keepdims=True) + jnp.exp(sink - m)
            r = _dot(p.astype(BF16), v) / l
            res.append(r if hh == kv_head else pltpu.roll(r, HEAD_DIM, axis=1))
        o_ref[:, c * LANES:(c + 1) * LANES] = jnp.where(low, res[0], res[1]).astype(o_ref.dtype)


def _sw(q, k, v, sink):
    b, s, w = q.shape
    kvw = k.shape[-1]
    tq = SW_Q_TILE
    per = tq // SW_WINDOW
    n_blk = s // SW_WINDOW
    q_spec = pl.BlockSpec((None, tq, w), lambda bi, i: (bi, i, 0))
    cur_spec = pl.BlockSpec((None, tq, kvw), lambda bi, i: (bi, i, 0))
    prev_spec = pl.BlockSpec((None, SW_WINDOW, kvw),
                             lambda bi, i: (bi, jnp.maximum(per * i - 1, 0), 0))
    next_spec = pl.BlockSpec((None, SW_WINDOW, kvw),
                             lambda bi, i: (bi, jnp.minimum(per * (i + 1), n_blk - 1), 0))
    sink_spec = pl.BlockSpec(memory_space=pltpu.SMEM)
    return pl.pallas_call(
        functools.partial(_sw_kernel, seq=s),
        grid=(b, s // tq),
        in_specs=[sink_spec, q_spec, prev_spec, cur_spec, next_spec, prev_spec, cur_spec, next_spec],
        out_specs=q_spec,
        out_shape=jax.ShapeDtypeStruct((b, s, w), BF16),
        compiler_params=pltpu.CompilerParams(dimension_semantics=("arbitrary", "arbitrary")),
        name="sliding_window_attn",
    )(sink, q, k, k, k, v, v, v)


def _diff_kernel(q_ref, k_ref, v_ref, lq1_ref, lk1_ref, lq2_ref, lk2_ref, g_ref, o_ref,
                 *, lam_init, kv_tile):
    tq = q_ref.shape[0]
    low = _low_half_mask()
    q = q_ref[...]
    zero = jnp.zeros_like(q)
    qz = jnp.concatenate([jnp.where(low, q, zero), jnp.where(low, zero, q)], axis=0)

    def step(j, carry):
        m, l, acc = carry
        start = pl.multiple_of(j * kv_tile, kv_tile)
        k = k_ref[pl.ds(start, kv_tile), :]
        v = v_ref[pl.ds(start, kv_tile), :]
        s = _dot_nt(qz, k)
        m_new = jnp.maximum(m, jnp.max(s, axis=-1, keepdims=True))
        alpha = jnp.exp(m - m_new)
        p = jnp.exp(s - m_new)
        l = alpha * l + jnp.sum(p, axis=-1, keepdims=True)
        acc = alpha * acc + _dot(p.astype(BF16), v)
        return m_new, l, acc

    init = (jnp.full((2 * tq, 1), -jnp.inf, F32), jnp.zeros((2 * tq, 1), F32),
            jnp.zeros((2 * tq, LANES), F32))
    _, l, acc = lax.fori_loop(0, k_ref.shape[0] // kv_tile, step, init)
    o = acc / l
    lam = (jnp.exp(jnp.sum(lq1_ref[...] * lk1_ref[...]))
           - jnp.exp(jnp.sum(lq2_ref[...] * lk2_ref[...])) + lam_init)
    o = o[:tq] - lam * o[tq:]
    o_ref[...] = (_rmsnorm(o, g_ref[...]) * (1.0 - lam_init)).astype(o_ref.dtype)


def _diff(q, k, v, lq1, lk1, lq2, lk2, subln_g, lam_init):
    b, s, w = q.shape
    tq = DIFF_Q_TILE
    q_spec = pl.BlockSpec((None, tq, LANES), lambda bi, h, i: (bi, i, h))
    kv_spec = pl.BlockSpec((None, s, LANES), lambda bi, h, i: (bi, 0, h))
    vec = lambda n: pl.BlockSpec((1, n), lambda bi, h, i: (0, 0))
    return pl.pallas_call(
        functools.partial(_diff_kernel, lam_init=lam_init, kv_tile=min(DIFF_KV_TILE, s)),
        grid=(b, w // LANES, s // tq),
        in_specs=[q_spec, kv_spec, kv_spec, vec(HEAD_DIM), vec(HEAD_DIM), vec(HEAD_DIM),
                  vec(HEAD_DIM), vec(LANES)],
        out_specs=q_spec,
        out_shape=jax.ShapeDtypeStruct((b, s, w), BF16),
        compiler_params=pltpu.CompilerParams(
            dimension_semantics=("arbitrary", "arbitrary", "arbitrary"),
            vmem_limit_bytes=32 * MIB),
        name="differential_attn",
    )(q, k, v, lq1, lk1, lq2, lk2, subln_g)


def _merge_kernel(h_ref, u_ref, oa_ref, ob_ref, oc_ref, wg_ref, bg_ref, wb_ref, wo_ref, g_ref,
                  o_ref):
    d = h_ref.shape[1]
    u = u_ref[...]
    merged = None
    for n, br_ref in enumerate((oa_ref, ob_ref, oc_ref)):
        gate = jax.nn.sigmoid(_dot(u, wg_ref[:, n * d:(n + 1) * d]) + bg_ref[n:n + 1, :])
        term = gate * _dot(br_ref[...], wb_ref[n])
        merged = term if merged is None else merged + term
    y = _dot(merged.astype(BF16), wo_ref[...])
    o_ref[...] = h_ref[...] + _rmsnorm(y, g_ref[...])


def _merge(h, u, oa, ob, oc, w_gate, b_gate, w_branch, w_out, post_g):
    t, d = h.shape
    rows = lambda w: pl.BlockSpec((TOKEN_TILE, w), lambda i: (i, 0))
    return pl.pallas_call(
        _merge_kernel,
        grid=(t // TOKEN_TILE,),
        in_specs=[rows(d), rows(d), rows(oa.shape[1]), rows(ob.shape[1]), rows(oc.shape[1]),
                  _resident(w_gate.shape), _resident(b_gate.shape), _resident(w_branch.shape),
                  _resident(w_out.shape), _resident((1, d))],
        out_specs=rows(d),
        out_shape=jax.ShapeDtypeStruct((t, d), F32),
        compiler_params=pltpu.CompilerParams(
            dimension_semantics=("arbitrary",), vmem_limit_bytes=40 * MIB),
        name="gated_merge",
    )(h, u, oa, ob, oc, w_gate, b_gate, w_branch, w_out, post_g)


def _rope_tables(s):
    pos = jnp.arange(s, dtype=F32)
    inv = ROPE_THETA ** (-jnp.arange(0, HEAD_DIM, 2, dtype=F32) / HEAD_DIM)
    ang = pos[:, None] * inv[None, :]
    ang = jnp.concatenate([ang, ang], axis=-1)
    cos = jnp.tile(jnp.cos(ang), (1, LANES // HEAD_DIM))
    sin = jnp.tile(jnp.sin(ang), (1, LANES // HEAD_DIM))
    first_half = (jnp.arange(LANES) % HEAD_DIM) < HEAD_DIM // 2
    return cos, jnp.where(first_half, -sin, 0.0), jnp.where(first_half, 0.0, sin)


def kernel(x, ffn1_pre_g, ffn1_w_gu, ffn1_w_down, ffn1_post_g, mix_pre_g, w_in, na_rpb, sw_sink, diff_lambda_q1, diff_lambda_k1, diff_lambda_q2, diff_lambda_k2, diff_subln_g, w_branch, b_gate, w_out, mix_post_g, ffn2_pre_g, ffn2_w_gu, ffn2_w_down, ffn2_post_g):
    b, s, d = x.shape
    depth = w_in.shape[0]
    assert s % (NA_ROWS_PER_STEP * GRID_W) == 0 and s % TOKEN_TILE == 0 and s >= NA_ROW_K * GRID_W
    cos, sin_lo, sin_hi = _rope_tables(s)
    row = lambda a: a.reshape(1, -1).astype(F32)
    h = x.reshape(b * s, d)
    for l in range(depth):
        h = _ffn(h, row(ffn1_pre_g[l]), ffn1_w_gu[l].astype(BF16), ffn1_w_down[l].astype(BF16),
                 row(ffn1_post_g[l]))

        w_l = w_in[l].astype(BF16)
        u, qa, ka, va, qb, kb, vb, qc, kc, vc = _proj(
            h, row(mix_pre_g[l]), w_l[:, :QKV_COLS], cos, sin_lo, sin_hi)
        seq = lambda a: a.reshape(b, s, a.shape[-1])
        oa = _na(seq(qa), seq(ka), seq(va), _na_bias_table(na_rpb[l]))
        ob = _sw(seq(qb), seq(kb), seq(vb), sw_sink[l].reshape(1, -1).astype(F32))
        lam_init = 0.8 - 0.6 * math.exp(-0.3 * l)
        oc = _diff(seq(qc), seq(kc), seq(vc), row(diff_lambda_q1[l]), row(diff_lambda_k1[l]),
                   row(diff_lambda_q2[l]), row(diff_lambda_k2[l]), row(diff_subln_g[l]), lam_init)
        flat = lambda a: a.reshape(b * s, a.shape[-1])
        h = _merge(h, u, flat(oa), flat(ob), flat(oc), w_l[:, QKV_COLS:], b_gate[l].astype(F32),
                   w_branch[l].astype(BF16), w_out[l].astype(BF16), row(mix_post_g[l]))

        h = _ffn(h, row(ffn2_pre_g[l]), ffn2_w_gu[l].astype(BF16), ffn2_w_down[l].astype(BF16),
                 row(ffn2_post_g[l]))
    return h.reshape(b, s, d)
```

```python
import functools
import math

import jax
import jax.numpy as jnp
from jax import lax
from jax.experimental import pallas as pl
from jax.experimental.pallas import tpu as pltpu

F32 = jnp.float32
BF16 = jnp.bfloat16

HEAD_DIM = 64
GRID_W = 64
NA_HEADS = 8
NA_ROW_K = 8
NA_COL_K = 16
SW_Q_HEADS = 8
SW_KV_HEADS = 2
SW_WINDOW = 128
DIFF_HEADS = 4
N_BRANCH = 3
ROPE_THETA = 10000.0
EPS = 1e-6
NEG = -1e30
LANES = 128
MIB = 1024 * 1024

A_W = NA_HEADS * HEAD_DIM
B_QW = SW_Q_HEADS * HEAD_DIM
B_KVW = SW_KV_HEADS * HEAD_DIM
C_W = 2 * DIFF_HEADS * HEAD_DIM
QKV_COLS = 3 * A_W + B_QW + 2 * B_KVW + 3 * C_W

TOKEN_TILE = 512
NA_ROWS_PER_STEP = 8
SW_Q_TILE = 256
DIFF_Q_TILE = 256
DIFF_KV_TILE = 512
DIFF_SCORE_ROWS = 256
DIFF_PIECE_ROWS = 16
DIFF_WARM_ROWS = 128
DIFF_REF_SLACK = 40.0
SUBLANES = 8


def _rmsnorm(x, g):
    return x * lax.rsqrt(jnp.mean(x * x, axis=-1, keepdims=True) + EPS) * g


def _dot(a, b):
    return jnp.dot(a, b, preferred_element_type=F32)


def _dot_nt(a, b):
    return lax.dot_general(a, b, (((1,), (1,)), ((), ())), preferred_element_type=F32)


def _resident(shape):
    return pl.BlockSpec(shape, lambda *_: (0,) * len(shape), pipeline_mode=pl.Buffered(1))


def _low_half_mask():
    return lax.broadcasted_iota(jnp.int32, (1, LANES), 1) < HEAD_DIM


def _ffn_kernel(x_ref, pre_g_ref, wgu_ref, wdown_ref, post_g_ref, o_ref, *, f_chunks, d_ff):
    x = x_ref[...]
    xn = _rmsnorm(x, pre_g_ref[...]).astype(BF16)
    acc = None
    for c0, c1 in f_chunks:
        gate = _dot(xn, wgu_ref[:, c0:c1])
        up = _dot(xn, wgu_ref[:, d_ff + c0:d_ff + c1])
        act = (gate * jax.nn.sigmoid(gate) * up).astype(BF16)
        part = _dot(act, wdown_ref[c0:c1, :])
        acc = part if acc is None else acc + part
    o_ref[...] = x + 0.5 * _rmsnorm(acc, post_g_ref[...])


def _ffn(h, pre_g, w_gu, w_down, post_g):
    t, d = h.shape
    d_ff = w_down.shape[0]
    step = 1024
    f_chunks = tuple((c, min(c + step, d_ff)) for c in range(0, d_ff, step))
    row = pl.BlockSpec((TOKEN_TILE, d), lambda i: (i, 0))
    return pl.pallas_call(
        functools.partial(_ffn_kernel, f_chunks=f_chunks, d_ff=d_ff),
        grid=(t // TOKEN_TILE,),
        in_specs=[row, _resident((1, d)), _resident(w_gu.shape), _resident(w_down.shape),
                  _resident((1, d))],
        out_specs=row,
        out_shape=jax.ShapeDtypeStruct((t, d), F32),
        compiler_params=pltpu.CompilerParams(
            dimension_semantics=("arbitrary",), vmem_limit_bytes=48 * MIB),
        name="ffn",
    )(h, pre_g, w_gu, w_down, post_g)


def _rope(t, cos, sin_lo, sin_hi):
    out = []
    for c in range(t.shape[1] // LANES):
        tc = t[:, c * LANES:(c + 1) * LANES]
        fwd = pltpu.roll(tc, LANES - HEAD_DIM // 2, axis=1)
        back = pltpu.roll(tc, HEAD_DIM // 2, axis=1)
        out.append(tc * cos + fwd * sin_lo + back * sin_hi)
    return jnp.concatenate(out, axis=1)


def _proj_kernel(h_ref, g_ref, w_ref, cos_ref, sin_lo_ref, sin_hi_ref,
                 u_ref, qa_ref, ka_ref, va_ref, qb_ref, kb_ref, vb_ref, qc_ref, kc_ref, vc_ref):
    un = _rmsnorm(h_ref[...], g_ref[...]).astype(BF16)
    u_ref[...] = un
    cos, sin_lo, sin_hi = cos_ref[...], sin_lo_ref[...], sin_hi_ref[...]
    scale = HEAD_DIM ** -0.5
    col = 0

    def seg(width):
        nonlocal col
        out = _dot(un, w_ref[:, col:col + width])
        col += width
        return out

    qa_ref[...] = (seg(A_W) * scale).astype(BF16)
    ka_ref[...] = seg(A_W).astype(BF16)
    va_ref[...] = seg(A_W).astype(BF16)
    qb_ref[...] = (_rope(seg(B_QW), cos, sin_lo, sin_hi) * scale).astype(BF16)
    kb_ref[...] = _rope(seg(B_KVW), cos, sin_lo, sin_hi).astype(BF16)
    vb_ref[...] = seg(B_KVW).astype(BF16)
    qc_ref[...] = (_rope(seg(C_W), cos, sin_lo, sin_hi) * (scale * math.log2(math.e))).astype(BF16)
    kc_ref[...] = _rope(seg(C_W), cos, sin_lo, sin_hi).astype(BF16)
    vc_ref[...] = seg(C_W).astype(BF16)


def _proj(h, g, w_qkv, cos, sin_lo, sin_hi):
    t, d = h.shape
    s = cos.shape[0]
    tiles_per_seq = s // TOKEN_TILE
    widths = (d, A_W, A_W, A_W, B_QW, B_KVW, B_KVW, C_W, C_W, C_W)
    rows = lambda w: pl.BlockSpec((TOKEN_TILE, w), lambda i: (i, 0))
    table = pl.BlockSpec((TOKEN_TILE, LANES), lambda i: (i % tiles_per_seq, 0))
    return pl.pallas_call(
        _proj_kernel,
        grid=(t // TOKEN_TILE,),
        in_specs=[rows(d), _resident((1, d)), _resident(w_qkv.shape), table, table, table],
        out_specs=[rows(w) for w in widths],
        out_shape=[jax.ShapeDtypeStruct((t, w), BF16) for w in widths],
        compiler_params=pltpu.CompilerParams(
            dimension_semantics=("arbitrary",), vmem_limit_bytes=40 * MIB),
        name="proj",
    )(h, g, w_qkv, cos, sin_lo, sin_hi)


def _na_kernel(q_ref, kp_ref, kc_ref, kn_ref, vp_ref, vc_ref, vn_ref, bias_ref, o_ref,
               kwin_ref, vwin_ref, *, grid_rows):
    j = pl.program_id(1)
    half = NA_ROWS_PER_STEP // 2 * GRID_W
    cur = NA_ROWS_PER_STEP * GRID_W
    kwin_ref[0:half, :] = kp_ref[...]
    kwin_ref[half:half + cur, :] = kc_ref[...]
    kwin_ref[half + cur:, :] = kn_ref[...]
    vwin_ref[0:half, :] = vp_ref[...]
    vwin_ref[half:half + cur, :] = vc_ref[...]
    vwin_ref[half + cur:, :] = vn_ref[...]
    low = _low_half_mask()
    keys = NA_ROW_K * GRID_W

    def one_row(i, carry):
        r = j * NA_ROWS_PER_STEP + i
        row_start = jnp.clip(r - NA_ROW_K // 2, 0, grid_rows - NA_ROW_K)
        off = pl.multiple_of((row_start - (j * NA_ROWS_PER_STEP - NA_ROWS_PER_STEP // 2)) * GRID_W,
                             GRID_W)
        dr0 = row_start - r + (NA_ROW_K - 1)
        q_row = q_ref[pl.ds(pl.multiple_of(i * GRID_W, GRID_W), GRID_W), :]
        chunks = []
        for c in range(A_W // LANES):
            k = kwin_ref[pl.ds(off, keys), c * LANES:(c + 1) * LANES]
            v = vwin_ref[pl.ds(off, keys), c * LANES:(c + 1) * LANES]
            qc = q_row[:, c * LANES:(c + 1) * LANES]
            res = []
            for hh in range(2):
                head = 2 * c + hh
                qz = jnp.where(low if hh == 0 else ~low, qc, jnp.zeros_like(qc))
                s = _dot_nt(qz, k)
                bias = jnp.concatenate(
                    [bias_ref[head, dr0 + 2 * t] for t in range(NA_ROW_K // 2)], axis=1)
                s = s + bias
                m = jnp.max(s, axis=-1, keepdims=True)
                p = jnp.exp(s - m)
                l = jnp.sum(p, axis=-1, keepdims=True)
                res.append(_dot(p.astype(BF16), v) / l)
            chunks.append(jnp.where(low, res[0], res[1]))
        o_ref[pl.ds(pl.multiple_of(i * GRID_W, GRID_W), GRID_W), :] = (
            jnp.concatenate(chunks, axis=1).astype(o_ref.dtype))
        return carry

    lax.fori_loop(0, NA_ROWS_PER_STEP, one_row, 0)


def _na_bias_table(rpb):
    c = jnp.arange(GRID_W)
    dc = jnp.clip(c[None, :] - c[:, None] + (NA_COL_K - 1), 0, 2 * NA_COL_K - 2)
    col_start = jnp.clip(c - NA_COL_K // 2, 0, GRID_W - NA_COL_K)
    col_ok = (c[None, :] >= col_start[:, None]) & (c[None, :] < col_start[:, None] + NA_COL_K)
    b = jnp.where(col_ok[None, None], rpb.astype(F32)[:, :, dc], NEG)
    return jnp.concatenate([b[:, :-1], b[:, 1:]], axis=-1)


def _na(q, k, v, bias):
    b, s, w = q.shape
    grid_rows = s // GRID_W
    cur = NA_ROWS_PER_STEP * GRID_W
    half = cur // 2
    n_half = s // half
    cur_spec = pl.BlockSpec((None, cur, w), lambda bi, j: (bi, j, 0))
    prev_spec = pl.BlockSpec((None, half, w), lambda bi, j: (bi, jnp.maximum(2 * j - 1, 0), 0))
    next_spec = pl.BlockSpec((None, half, w),
                             lambda bi, j: (bi, jnp.minimum(2 * j + 2, n_half - 1), 0))
    return pl.pallas_call(
        functools.partial(_na_kernel, grid_rows=grid_rows),
        grid=(b, s // cur),
        in_specs=[cur_spec, prev_spec, cur_spec, next_spec, prev_spec, cur_spec, next_spec,
                  _resident(bias.shape)],
        out_specs=cur_spec,
        out_shape=jax.ShapeDtypeStruct((b, s, w), BF16),
        scratch_shapes=[pltpu.VMEM((2 * cur, w), BF16), pltpu.VMEM((2 * cur, w), BF16)],
        compiler_params=pltpu.CompilerParams(
            dimension_semantics=("arbitrary", "arbitrary"), vmem_limit_bytes=32 * MIB),
        name="neighbourhood_attn",
    )(q, k, k, k, v, v, v, bias)


def _sw_kernel(sink_ref, q_ref, kp_ref, kc_ref, kn_ref, vp_ref, vc_ref, vn_ref, o_ref, *, seq):
    i = pl.program_id(1)
    tq = q_ref.shape[0]
    k = jnp.concatenate([kp_ref[...], kc_ref[...], kn_ref[...]], axis=0)
    v = jnp.concatenate([vp_ref[...], vc_ref[...], vn_ref[...]], axis=0)
    nk = k.shape[0]
    qpos = i * tq + lax.broadcasted_iota(jnp.int32, (tq, nk), 0)
    kpos = i * tq - SW_WINDOW + lax.broadcasted_iota(jnp.int32, (tq, nk), 1)
    ok = (jnp.abs(qpos - kpos) <= SW_WINDOW) & (kpos >= 0) & (kpos < seq)
    lane = lax.broadcasted_iota(jnp.int32, (1, LANES), 1)
    low = lane < HEAD_DIM
    group = SW_Q_HEADS // SW_KV_HEADS
    q = q_ref[...]
    for c in range(B_QW // LANES):
        qc = q[:, c * LANES:(c + 1) * LANES]
        qc_swapped = pltpu.roll(qc, HEAD_DIM, axis=1)
        res = []
        for hh in range(2):
            head = 2 * c + hh
            kv_head = head // group
            src = qc if hh == kv_head else qc_swapped
            qz = jnp.where(low if kv_head == 0 else ~low, src, jnp.zeros_like(src))
            s = jnp.where(ok, _dot_nt(qz, k), NEG)
            sink = sink_ref[0, head]
            m = jnp.maximum(jnp.max(s, axis=-1, keepdims=True), sink)
            p = jnp.exp(s - m)
            l = jnp.sum(p, axis=-1, keepdims=True) + jnp.exp(sink - m)
            r = _dot(p.astype(BF16), v) / l
            res.append(r if hh == kv_head else pltpu.roll(r, HEAD_DIM, axis=1))
        o_ref[:, c * LANES:(c + 1) * LANES] = jnp.where(low, res[0], res[1]).astype(o_ref.dtype)


def _sw(q, k, v, sink):
    b, s, w = q.shape
    kvw = k.shape[-1]
    tq = SW_Q_TILE
    per = tq // SW_WINDOW
    n_blk = s // SW_WINDOW
    q_spec = pl.BlockSpec((None, tq, w), lambda bi, i: (bi, i, 0))
    cur_spec = pl.BlockSpec((None, tq, kvw), lambda bi, i: (bi, i, 0))
    prev_spec = pl.BlockSpec((None, SW_WINDOW, kvw),
                             lambda bi, i: (bi, jnp.maximum(per * i - 1, 0), 0))
    next_spec = pl.BlockSpec((None, SW_WINDOW, kvw),
                             lambda bi, i: (bi, jnp.minimum(per * (i + 1), n_blk - 1), 0))
    sink_spec = pl.BlockSpec(memory_space=pltpu.SMEM)
    return pl.pallas_call(
        functools.partial(_sw_kernel, seq=s),
        grid=(b, s // tq),
        in_specs=[sink_spec, q_spec, prev_spec, cur_spec, next_spec, prev_spec, cur_spec, next_spec],
        out_specs=q_spec,
        out_shape=jax.ShapeDtypeStruct((b, s, w), BF16),
        compiler_params=pltpu.CompilerParams(dimension_semantics=("arbitrary", "arbitrary")),
        name="sliding_window_attn",
    )(sink, q, k, k, k, v, v, v)


def _diff_kernel(q_ref, k_ref, vt_ref, lq1_ref, lk1_ref, lq2_ref, lk2_ref, g_ref, o_ref,
                 m_ref, l_ref, acc_ref, p0_ref, p1_ref, qz_ref, *, lam_init):
    tq = q_ref.shape[0]
    n_kv, _, kv_tile = vt_ref.shape
    cols = 2 * tq
    p_bufs = (p0_ref, p1_ref)
    low = _low_half_mask()
    q = q_ref[...]
    zero = jnp.zeros_like(q)
    qz_ref[0:tq, :] = jnp.where(low, q, zero)
    qz_ref[tq:, :] = jnp.where(low, zero, q)

    def fold8(x, op):
        return op(x.reshape(x.shape[0] // SUBLANES, SUBLANES, x.shape[1]), axis=0)

    def col_max(x8):
        return jnp.max(x8, axis=0, keepdims=True)

    ref = col_max(fold8(_dot_nt(k_ref[0:DIFF_WARM_ROWS, :], qz_ref[...]), jnp.max))
    unsafe = jnp.zeros((1, cols), F32)
    l8 = None
    beta = None
    pending = None

    def value_product(tile, scale):
        pv = _dot(vt_ref[tile], p_bufs[tile % 2][...])
        acc_ref[...] = pv if scale is None else scale * acc_ref[...] + pv

    for t in range(n_kv):
        tile_max8 = None
        tile_sum8 = None
        n_chunks = kv_tile // DIFF_SCORE_ROWS
        for c in range(n_chunks):
            start = t * kv_tile + c * DIFF_SCORE_ROWS
            s = _dot_nt(k_ref[start:start + DIFF_SCORE_ROWS, :], qz_ref[...])
            if c == n_chunks // 2 and pending is not None:
                value_product(*pending)
                pending = None
            for r in range(0, DIFF_SCORE_ROWS, DIFF_PIECE_ROWS):
                piece = s[r:r + DIFF_PIECE_ROWS]
                mx = fold8(piece, jnp.max)
                tile_max8 = mx if tile_max8 is None else jnp.maximum(tile_max8, mx)
                p = jnp.exp2(piece - ref)
                ps = fold8(p, jnp.sum)
                tile_sum8 = ps if tile_sum8 is None else tile_sum8 + ps
                row = c * DIFF_SCORE_ROWS + r
                p_bufs[t % 2][row:row + DIFF_PIECE_ROWS, :] = p.astype(BF16)
        if pending is not None:
            value_product(*pending)
        pending = (t, beta)
        l8 = tile_sum8 if beta is None else beta * l8 + tile_sum8
        tile_max = col_max(tile_max8)
        unsafe = jnp.maximum(unsafe, jnp.where(tile_max - ref > DIFF_REF_SLACK, 1.0, 0.0))
        new_ref = jnp.maximum(ref, tile_max)
        beta = jnp.exp2(ref - new_ref)
        ref = new_ref
    value_product(*pending)
    l_fast = jnp.sum(l8, axis=0, keepdims=True)
    l_ref[...] = l_fast
    finite = jnp.isfinite(jnp.sum(acc_ref[...], axis=0, keepdims=True) + l_fast)
    unsafe = jnp.maximum(unsafe, jnp.where(finite, 0.0, 1.0))

    @pl.when(jnp.max(unsafe) > 0.0)
    def _():
        m_ref[...] = jnp.full(m_ref.shape, -jnp.inf, F32)
        l_ref[...] = jnp.zeros(l_ref.shape, F32)
        acc_ref[...] = jnp.zeros(acc_ref.shape, F32)

        def tile_step(j, carry):
            k = k_ref[pl.ds(pl.multiple_of(j * kv_tile, kv_tile), kv_tile), :]
            s = _dot_nt(k, qz_ref[...])
            m_old = m_ref[...]
            m_new = jnp.maximum(m_old, jnp.max(s, axis=0, keepdims=True))
            alpha = jnp.exp2(m_old - m_new)
            p = jnp.exp2(s - m_new)
            m_ref[...] = m_new
            l_ref[...] = alpha * l_ref[...] + jnp.sum(p, axis=0, keepdims=True)
            acc_ref[...] = alpha * acc_ref[...] + _dot(vt_ref[j], p.astype(BF16))
            return carry

        lax.fori_loop(0, n_kv, tile_step, 0)


    o = acc_ref[...] / l_ref[...]
    lam = (jnp.exp(jnp.sum(lq1_ref[...] * lk1_ref[...]))
           - jnp.exp(jnp.sum(lq2_ref[...] * lk2_ref[...])) + lam_init)
    o = o[:, :tq] - lam * o[:, tq:]
    o = o * lax.rsqrt(jnp.mean(o * o, axis=0, keepdims=True) + EPS) * g_ref[...]
    o_ref[...] = (o * (1.0 - lam_init)).T.astype(o_ref.dtype)


def _diff(q, k, vt, lq1, lk1, lq2, lk2, subln_g, lam_init):
    b, s, w = q.shape
    tq = DIFF_Q_TILE
    _, heads, n_kv, dv, kv_tile = vt.shape
    q_spec = pl.BlockSpec((None, tq, LANES), lambda bi, h, i: (bi, i, h))
    k_spec = pl.BlockSpec((None, s, LANES), lambda bi, h, i: (bi, 0, h))
    vt_spec = pl.BlockSpec((None, None, n_kv, dv, kv_tile), lambda bi, h, i: (bi, h, 0, 0, 0))
    vec = lambda n: pl.BlockSpec((1, n), lambda bi, h, i: (0, 0))
    return pl.pallas_call(
        functools.partial(_diff_kernel, lam_init=lam_init),
        grid=(b, heads, s // tq),
        in_specs=[q_spec, k_spec, vt_spec, vec(HEAD_DIM), vec(HEAD_DIM), vec(HEAD_DIM),
                  vec(HEAD_DIM), pl.BlockSpec((dv, 1), lambda bi, h, i: (0, 0))],
        out_specs=q_spec,
        out_shape=jax.ShapeDtypeStruct((b, s, w), BF16),
        scratch_shapes=[pltpu.VMEM((1, 2 * tq), F32), pltpu.VMEM((1, 2 * tq), F32),
                        pltpu.VMEM((dv, 2 * tq), F32),
                        pltpu.VMEM((kv_tile, 2 * tq), BF16), pltpu.VMEM((kv_tile, 2 * tq), BF16),
                        pltpu.VMEM((2 * tq, LANES), BF16)],
        compiler_params=pltpu.CompilerParams(
            dimension_semantics=("arbitrary", "arbitrary", "arbitrary"),
            vmem_limit_bytes=32 * MIB),
        name="differential_attn",
    )(q, k, vt, lq1, lk1, lq2, lk2, subln_g)


def _merge_kernel(h_ref, u_ref, oa_ref, ob_ref, oc_ref, wg_ref, bg_ref, wb_ref, wo_ref, g_ref,
                  o_ref):
    d = h_ref.shape[1]
    u = u_ref[...]
    merged = None
    for n, br_ref in enumerate((oa_ref, ob_ref, oc_ref)):
        gate = jax.nn.sigmoid(_dot(u, wg_ref[:, n * d:(n + 1) * d]) + bg_ref[n:n + 1, :])
        term = gate * _dot(br_ref[...], wb_ref[n])
        merged = term if merged is None else merged + term
    y = _dot(merged.astype(BF16), wo_ref[...])
    o_ref[...] = h_ref[...] + _rmsnorm(y, g_ref[...])


def _merge(h, u, oa, ob, oc, w_gate, b_gate, w_branch, w_out, post_g):
    t, d = h.shape
    rows = lambda w: pl.BlockSpec((TOKEN_TILE, w), lambda i: (i, 0))
    return pl.pallas_call(
        _merge_kernel,
        grid=(t // TOKEN_TILE,),
        in_specs=[rows(d), rows(d), rows(oa.shape[1]), rows(ob.shape[1]), rows(oc.shape[1]),
                  _resident(w_gate.shape), _resident(b_gate.shape), _resident(w_branch.shape),
                  _resident(w_out.shape), _resident((1, d))],
        out_specs=rows(d),
        out_shape=jax.ShapeDtypeStruct((t, d), F32),
        compiler_params=pltpu.CompilerParams(
            dimension_semantics=("arbitrary",), vmem_limit_bytes=40 * MIB),
        name="gated_merge",
    )(h, u, oa, ob, oc, w_gate, b_gate, w_branch, w_out, post_g)


def _rope_tables(s):
    pos = jnp.arange(s, dtype=F32)
    inv = ROPE_THETA ** (-jnp.arange(0, HEAD_DIM, 2, dtype=F32) / HEAD_DIM)
    ang = pos[:, None] * inv[None, :]
    ang = jnp.concatenate([ang, ang], axis=-1)
    cos = jnp.tile(jnp.cos(ang), (1, LANES // HEAD_DIM))
    sin = jnp.tile(jnp.sin(ang), (1, LANES // HEAD_DIM))
    first_half = (jnp.arange(LANES) % HEAD_DIM) < HEAD_DIM // 2
    return cos, jnp.where(first_half, -sin, 0.0), jnp.where(first_half, 0.0, sin)


def kernel(x, ffn1_pre_g, ffn1_w_gu, ffn1_w_down, ffn1_post_g, mix_pre_g, w_in, na_rpb, sw_sink, diff_lambda_q1, diff_lambda_k1, diff_lambda_q2, diff_lambda_k2, diff_subln_g, w_branch, b_gate, w_out, mix_post_g, ffn2_pre_g, ffn2_w_gu, ffn2_w_down, ffn2_post_g):
    b, s, d = x.shape
    depth = w_in.shape[0]
    assert s % (NA_ROWS_PER_STEP * GRID_W) == 0 and s % TOKEN_TILE == 0 and s >= NA_ROW_K * GRID_W
    cos, sin_lo, sin_hi = _rope_tables(s)
    row = lambda a: a.reshape(1, -1).astype(F32)
    h = x.reshape(b * s, d)
    for l in range(depth):
        h = _ffn(h, row(ffn1_pre_g[l]), ffn1_w_gu[l].astype(BF16), ffn1_w_down[l].astype(BF16),
                 row(ffn1_post_g[l]))

        w_l = w_in[l].astype(BF16)
        u, qa, ka, va, qb, kb, vb, qc, kc, vc = _proj(
            h, row(mix_pre_g[l]), w_l[:, :QKV_COLS], cos, sin_lo, sin_hi)
        seq = lambda a: a.reshape(b, s, a.shape[-1])
        oa = _na(seq(qa), seq(ka), seq(va), _na_bias_table(na_rpb[l]))
        ob = _sw(seq(qb), seq(kb), seq(vb), sw_sink[l].reshape(1, -1).astype(F32))
        lam_init = 0.8 - 0.6 * math.exp(-0.3 * l)
        kv_tile = min(DIFF_KV_TILE, s)
        vct = vc.reshape(b, s // kv_tile, kv_tile, DIFF_HEADS, 2 * HEAD_DIM).transpose(0, 3, 1, 4, 2)
        oc = _diff(seq(qc), seq(kc), vct, row(diff_lambda_q1[l]), row(diff_lambda_k1[l]),
                   row(diff_lambda_q2[l]), row(diff_lambda_k2[l]),
                   diff_subln_g[l].reshape(-1, 1).astype(F32), lam_init)
        flat = lambda a: a.reshape(b * s, a.shape[-1])
        h = _merge(h, u, flat(oa), flat(ob), flat(oc), w_l[:, QKV_COLS:], b_gate[l].astype(F32),
                   w_branch[l].astype(BF16), w_out[l].astype(BF16), row(mix_post_g[l]))

        h = _ffn(h, row(ffn2_pre_g[l]), ffn2_w_gu[l].astype(BF16), ffn2_w_down[l].astype(BF16),
                 row(ffn2_post_g[l]))
    return h.reshape(b, s, d)
```

```python
import functools
import math

import jax
import jax.numpy as jnp
from jax import lax
from jax.experimental import pallas as pl
from jax.experimental.pallas import tpu as pltpu

F32 = jnp.float32
BF16 = jnp.bfloat16

HEAD_DIM = 64
GRID_W = 64
NA_HEADS = 8
NA_ROW_K = 8
NA_COL_K = 16
SW_Q_HEADS = 8
SW_KV_HEADS = 2
SW_WINDOW = 128
DIFF_HEADS = 4
N_BRANCH = 3
ROPE_THETA = 10000.0
EPS = 1e-6
NEG = -1e30
LANES = 128
MIB = 1024 * 1024

A_W = NA_HEADS * HEAD_DIM
B_QW = SW_Q_HEADS * HEAD_DIM
B_KVW = SW_KV_HEADS * HEAD_DIM
C_W = 2 * DIFF_HEADS * HEAD_DIM
QKV_COLS = 3 * A_W + B_QW + 2 * B_KVW + 3 * C_W

TOKEN_TILE = 512
NA_ROWS_PER_STEP = 4
SW_Q_TILE = 256
DIFF_Q_TILE = 256
DIFF_KV_TILE = 512
DIFF_SCORE_ROWS = 256
DIFF_PIECE_ROWS = 16
DIFF_WARM_ROWS = 128
DIFF_REF_SLACK = 40.0
SUBLANES = 8


def _rmsnorm(x, g):
    return x * lax.rsqrt(jnp.mean(x * x, axis=-1, keepdims=True) + EPS) * g


def _dot(a, b):
    return jnp.dot(a, b, preferred_element_type=F32)


def _dot_nt(a, b):
    return lax.dot_general(a, b, (((1,), (1,)), ((), ())), preferred_element_type=F32)


def _resident(shape):
    return pl.BlockSpec(shape, lambda *_: (0,) * len(shape), pipeline_mode=pl.Buffered(1))


def _low_half_mask():
    return lax.broadcasted_iota(jnp.int32, (1, LANES), 1) < HEAD_DIM


def _head_pipeline(heads, scores, attend):
    heads = list(heads)
    out = []
    s_next = scores(heads[0])
    for idx, head in enumerate(heads):
        s = s_next
        if idx + 1 < len(heads):
            s_next = scores(heads[idx + 1])
        out.append(attend(head, s))
    return out


def _ffn_kernel(x_ref, pre_g_ref, wgu_ref, wdown_ref, post_g_ref, o_ref, *, f_chunks, d_ff):
    x = x_ref[...]
    xn = _rmsnorm(x, pre_g_ref[...]).astype(BF16)
    acc = None
    for c0, c1 in f_chunks:
        gate = _dot(xn, wgu_ref[:, c0:c1])
        up = _dot(xn, wgu_ref[:, d_ff + c0:d_ff + c1])
        act = (gate * jax.nn.sigmoid(gate) * up).astype(BF16)
        part = _dot(act, wdown_ref[c0:c1, :])
        acc = part if acc is None else acc + part
    o_ref[...] = x + 0.5 * _rmsnorm(acc, post_g_ref[...])


def _ffn(h, pre_g, w_gu, w_down, post_g):
    t, d = h.shape
    d_ff = w_down.shape[0]
    step = 1024
    f_chunks = tuple((c, min(c + step, d_ff)) for c in range(0, d_ff, step))
    row = pl.BlockSpec((TOKEN_TILE, d), lambda i: (i, 0))
    return pl.pallas_call(
        functools.partial(_ffn_kernel, f_chunks=f_chunks, d_ff=d_ff),
        grid=(t // TOKEN_TILE,),
        in_specs=[row, _resident((1, d)), _resident(w_gu.shape), _resident(w_down.shape),
                  _resident((1, d))],
        out_specs=row,
        out_shape=jax.ShapeDtypeStruct((t, d), F32),
        compiler_params=pltpu.CompilerParams(
            dimension_semantics=("arbitrary",), vmem_limit_bytes=48 * MIB),
        name="ffn",
    )(h, pre_g, w_gu, w_down, post_g)


def _rope(t, cos, sin_lo, sin_hi):
    out = []
    for c in range(t.shape[1] // LANES):
        tc = t[:, c * LANES:(c + 1) * LANES]
        fwd = pltpu.roll(tc, LANES - HEAD_DIM // 2, axis=1)
        back = pltpu.roll(tc, HEAD_DIM // 2, axis=1)
        out.append(tc * cos + fwd * sin_lo + back * sin_hi)
    return jnp.concatenate(out, axis=1)


def _proj_kernel(h_ref, g_ref, w_ref, cos_ref, sin_lo_ref, sin_hi_ref,
                 u_ref, qa_ref, ka_ref, va_ref, qb_ref, kb_ref, vb_ref, qc_ref, kc_ref, vc_ref):
    un = _rmsnorm(h_ref[...], g_ref[...]).astype(BF16)
    u_ref[...] = un
    cos, sin_lo, sin_hi = cos_ref[...], sin_lo_ref[...], sin_hi_ref[...]
    scale = HEAD_DIM ** -0.5
    col = 0

    def seg(width):
        nonlocal col
        out = _dot(un, w_ref[:, col:col + width])
        col += width
        return out

    qa_ref[...] = (seg(A_W) * scale).astype(BF16)
    ka_ref[...] = seg(A_W).astype(BF16)
    va_ref[...] = seg(A_W).astype(BF16)
    qb_ref[...] = (_rope(seg(B_QW), cos, sin_lo, sin_hi) * scale).astype(BF16)
    kb_ref[...] = _rope(seg(B_KVW), cos, sin_lo, sin_hi).astype(BF16)
    vb_ref[...] = seg(B_KVW).astype(BF16)
    qc_ref[...] = (_rope(seg(C_W), cos, sin_lo, sin_hi) * (scale * math.log2(math.e))).astype(BF16)
    kc_ref[...] = _rope(seg(C_W), cos, sin_lo, sin_hi).astype(BF16)
    vc_ref[...] = seg(C_W).astype(BF16)


def _proj(h, g, w_qkv, cos, sin_lo, sin_hi):
    t, d = h.shape
    s = cos.shape[0]
    tiles_per_seq = s // TOKEN_TILE
    widths = (d, A_W, A_W, A_W, B_QW, B_KVW, B_KVW, C_W, C_W, C_W)
    rows = lambda w: pl.BlockSpec((TOKEN_TILE, w), lambda i: (i, 0))
    table = pl.BlockSpec((TOKEN_TILE, LANES), lambda i: (i % tiles_per_seq, 0))
    return pl.pallas_call(
        _proj_kernel,
        grid=(t // TOKEN_TILE,),
        in_specs=[rows(d), _resident((1, d)), _resident(w_qkv.shape), table, table, table],
        out_specs=[rows(w) for w in widths],
        out_shape=[jax.ShapeDtypeStruct((t, w), BF16) for w in widths],
        compiler_params=pltpu.CompilerParams(
            dimension_semantics=("arbitrary",), vmem_limit_bytes=40 * MIB),
        name="proj",
    )(h, g, w_qkv, cos, sin_lo, sin_hi)


def _na_kernel(q_ref, kp_ref, kc_ref, kn_ref, vtp_ref, vtc_ref, vtn_ref, bias_ref, o_ref):
    blk = q_ref.shape[0]
    k_refs = (kp_ref, kc_ref, kn_ref)
    vt_refs = (vtp_ref, vtc_ref, vtn_ref)
    low = _low_half_mask()
    top = lax.broadcasted_iota(jnp.int32, (LANES, 1), 0) < HEAD_DIM
    lanes = lambda head: slice(head // 2 * LANES, (head // 2 + 1) * LANES)

    def scores(head):
        qc = q_ref[:, lanes(head)]
        qz = jnp.where(low if head % 2 == 0 else ~low, qc, jnp.zeros_like(qc))
        return [_dot_nt(k_refs[w][:, lanes(head)], qz) + bias_ref[head, w * blk:(w + 1) * blk, :]
                for w in range(3)]

    def attend(head, s):
        m = jnp.max(jnp.maximum(jnp.maximum(s[0], s[1]), s[2]), axis=0, keepdims=True)
        p = [jnp.exp(sw - m) for sw in s]
        l = jnp.sum(p[0] + p[1] + p[2], axis=0, keepdims=True)
        o = sum(_dot(vt_refs[w][lanes(head), :], p[w].astype(BF16)) for w in range(3))
        return o / l

    res = _head_pipeline(range(NA_HEADS), scores, attend)
    for c in range(A_W // LANES):
        o_ref[:, c * LANES:(c + 1) * LANES] = (
            jnp.where(top, res[2 * c], res[2 * c + 1]).T.astype(o_ref.dtype))


def _na_bias_table(rpb, grid_rows):
    n = NA_ROWS_PER_STEP
    c = jnp.arange(GRID_W)
    dc = jnp.clip(c[:, None] - c[None, :] + (NA_COL_K - 1), 0, 2 * NA_COL_K - 2)
    col_start = jnp.clip(c - NA_COL_K // 2, 0, GRID_W - NA_COL_K)
    col_ok = (c[:, None] >= col_start[None, :]) & (c[:, None] < col_start[None, :] + NA_COL_K)
    kr = jnp.arange(3 * n)[:, None]
    qr = jnp.arange(n)[None, :]
    dr = kr - qr + (NA_ROW_K - 1 - n)
    tables = []
    for first_row in (0, n, grid_rows - n):
        q_abs = first_row + qr
        k_abs = first_row - n + kr
        row_start = jnp.clip(q_abs - NA_ROW_K // 2, 0, grid_rows - NA_ROW_K)
        row_ok = (k_abs >= row_start) & (k_abs < row_start + NA_ROW_K)
        vals = rpb.astype(F32)[:, jnp.clip(dr, 0, 2 * NA_ROW_K - 2)][:, :, :, dc]
        ok = row_ok[:, :, None, None] & col_ok[None, None, :, :]
        t = jnp.where(ok[None], vals, NEG).transpose(0, 1, 3, 2, 4)
        tables.append(t.reshape(rpb.shape[0], 3 * n * GRID_W, n * GRID_W))
    return jnp.stack(tables)


def _na(q, k, vt, bias):
    b, s, w = q.shape
    blk = NA_ROWS_PER_STEP * GRID_W
    n_blk = s // blk
    clamp = lambda j: jnp.clip(j, 0, n_blk - 1)
    tok = lambda shift: pl.BlockSpec((None, blk, w), lambda bi, j: (bi, clamp(j + shift), 0))
    chan = lambda shift: pl.BlockSpec((None, w, blk), lambda bi, j: (bi, 0, clamp(j + shift)))
    table = pl.BlockSpec(
        (None,) + bias.shape[1:],
        lambda bi, j: (jnp.where(j == 0, 0, jnp.where(j == n_blk - 1, 2, 1)), 0, 0, 0))
    return pl.pallas_call(
        _na_kernel,
        grid=(b, n_blk),
        in_specs=[tok(0), tok(-1), tok(0), tok(1), chan(-1), chan(0), chan(1), table],
        out_specs=tok(0),
        out_shape=jax.ShapeDtypeStruct((b, s, w), BF16),
        compiler_params=pltpu.CompilerParams(
            dimension_semantics=("arbitrary", "arbitrary"), vmem_limit_bytes=40 * MIB),
        name="neighbourhood_attn",
    )(q, k, k, k, vt, vt, vt, bias)


def _sw_kernel(sink_ref, q_ref, kp_ref, kc_ref, kn_ref, vp_ref, vc_ref, vn_ref, o_ref, *, seq):
    i = pl.program_id(1)
    tq = q_ref.shape[0]
    k = jnp.concatenate([kp_ref[...], kc_ref[...], kn_ref[...]], axis=0)
    vt = jnp.concatenate([vp_ref[...], vc_ref[...], vn_ref[...]], axis=1)
    nk = k.shape[0]
    kpos = i * tq - SW_WINDOW + lax.broadcasted_iota(jnp.int32, (nk, tq), 0)
    qpos = i * tq + lax.broadcasted_iota(jnp.int32, (nk, tq), 1)
    ok = (jnp.abs(qpos - kpos) <= SW_WINDOW) & (kpos >= 0) & (kpos < seq)
    low = _low_half_mask()
    group = SW_Q_HEADS // SW_KV_HEADS

    def scores(head):
        kv_head = head // group
        qc = q_ref[:, head // 2 * LANES:(head // 2 + 1) * LANES]
        src = qc if head % 2 == kv_head else pltpu.roll(qc, HEAD_DIM, axis=1)
        qz = jnp.where(low if kv_head == 0 else ~low, src, jnp.zeros_like(src))
        return jnp.where(ok, _dot_nt(k, qz), NEG)

    def attend(head, s):
        kv_head = head // group
        sink = sink_ref[0, head]
        m = jnp.maximum(jnp.max(s, axis=0, keepdims=True), sink)
        p = jnp.exp(s - m)
        l = jnp.sum(p, axis=0, keepdims=True) + jnp.exp(sink - m)
        o = _dot(vt, p.astype(BF16)) / l
        return o[kv_head * HEAD_DIM:(kv_head + 1) * HEAD_DIM]

    res = _head_pipeline(range(SW_Q_HEADS), scores, attend)
    for c in range(B_QW // LANES):
        o_ref[:, c * LANES:(c + 1) * LANES] = (
            jnp.concatenate(res[2 * c:2 * c + 2], axis=0).T.astype(o_ref.dtype))


def _sw(q, k, vt, sink):
    b, s, w = q.shape
    kvw = k.shape[-1]
    tq = SW_Q_TILE
    per = tq // SW_WINDOW
    n_blk = s // SW_WINDOW
    prev = lambda i: jnp.maximum(per * i - 1, 0)
    nxt = lambda i: jnp.minimum(per * (i + 1), n_blk - 1)
    q_spec = pl.BlockSpec((None, tq, w), lambda bi, i: (bi, i, 0))
    k_specs = [pl.BlockSpec((None, SW_WINDOW, kvw), lambda bi, i: (bi, prev(i), 0)),
               pl.BlockSpec((None, tq, kvw), lambda bi, i: (bi, i, 0)),
               pl.BlockSpec((None, SW_WINDOW, kvw), lambda bi, i: (bi, nxt(i), 0))]
    vt_specs = [pl.BlockSpec((None, kvw, SW_WINDOW), lambda bi, i: (bi, 0, prev(i))),
                pl.BlockSpec((None, kvw, tq), lambda bi, i: (bi, 0, i)),
                pl.BlockSpec((None, kvw, SW_WINDOW), lambda bi, i: (bi, 0, nxt(i)))]
    sink_spec = pl.BlockSpec(memory_space=pltpu.SMEM)
    return pl.pallas_call(
        functools.partial(_sw_kernel, seq=s),
        grid=(b, s // tq),
        in_specs=[sink_spec, q_spec] + k_specs + vt_specs,
        out_specs=q_spec,
        out_shape=jax.ShapeDtypeStruct((b, s, w), BF16),
        compiler_params=pltpu.CompilerParams(dimension_semantics=("arbitrary", "arbitrary")),
        name="sliding_window_attn",
    )(sink, q, k, k, k, vt, vt, vt)


def _diff_kernel(q_ref, k_ref, vt_ref, lq1_ref, lk1_ref, lq2_ref, lk2_ref, g_ref, o_ref,
                 m_ref, l_ref, acc_ref, p0_ref, p1_ref, qz_ref, *, lam_init):
    tq = q_ref.shape[0]
    n_kv, _, kv_tile = vt_ref.shape
    cols = 2 * tq
    p_bufs = (p0_ref, p1_ref)
    low = _low_half_mask()
    q = q_ref[...]
    zero = jnp.zeros_like(q)
    qz_ref[0:tq, :] = jnp.where(low, q, zero)
    qz_ref[tq:, :] = jnp.where(low, zero, q)

    def fold8(x, op):
        return op(x.reshape(x.shape[0] // SUBLANES, SUBLANES, x.shape[1]), axis=0)

    def col_max(x8):
        return jnp.max(x8, axis=0, keepdims=True)

    ref = col_max(fold8(_dot_nt(k_ref[0:DIFF_WARM_ROWS, :], qz_ref[...]), jnp.max))
    unsafe = jnp.zeros((1, cols), F32)
    l8 = None
    beta = None
    pending = None

    def value_product(tile, scale):
        pv = _dot(vt_ref[tile], p_bufs[tile % 2][...])
        acc_ref[...] = pv if scale is None else scale * acc_ref[...] + pv

    for t in range(n_kv):
        tile_max8 = None
        tile_sum8 = None
        n_chunks = kv_tile // DIFF_SCORE_ROWS
        for c in range(n_chunks):
            start = t * kv_tile + c * DIFF_SCORE_ROWS
            s = _dot_nt(k_ref[start:start + DIFF_SCORE_ROWS, :], qz_ref[...])
            if c == n_chunks // 2 and pending is not None:
                value_product(*pending)
                pending = None
            for r in range(0, DIFF_SCORE_ROWS, DIFF_PIECE_ROWS):
                piece = s[r:r + DIFF_PIECE_ROWS]
                mx = fold8(piece, jnp.max)
                tile_max8 = mx if tile_max8 is None else jnp.maximum(tile_max8, mx)
                p = jnp.exp2(piece - ref)
                ps = fold8(p, jnp.sum)
                tile_sum8 = ps if tile_sum8 is None else tile_sum8 + ps
                row = c * DIFF_SCORE_ROWS + r
                p_bufs[t % 2][row:row + DIFF_PIECE_ROWS, :] = p.astype(BF16)
        if pending is not None:
            value_product(*pending)
        pending = (t, beta)
        l8 = tile_sum8 if beta is None else beta * l8 + tile_sum8
        tile_max = col_max(tile_max8)
        unsafe = jnp.maximum(unsafe, jnp.where(tile_max - ref > DIFF_REF_SLACK, 1.0, 0.0))
        new_ref = jnp.maximum(ref, tile_max)
        beta = jnp.exp2(ref - new_ref)
        ref = new_ref
    value_product(*pending)
    l_fast = jnp.sum(l8, axis=0, keepdims=True)
    l_ref[...] = l_fast
    finite = jnp.isfinite(jnp.sum(acc_ref[...], axis=0, keepdims=True) + l_fast)
    unsafe = jnp.maximum(unsafe, jnp.where(finite, 0.0, 1.0))

    @pl.when(jnp.max(unsafe) > 0.0)
    def _():
        m_ref[...] = jnp.full(m_ref.shape, -jnp.inf, F32)
        l_ref[...] = jnp.zeros(l_ref.shape, F32)
        acc_ref[...] = jnp.zeros(acc_ref.shape, F32)

        def tile_step(j, carry):
            k = k_ref[pl.ds(pl.multiple_of(j * kv_tile, kv_tile), kv_tile), :]
            s = _dot_nt(k, qz_ref[...])
            m_old = m_ref[...]
            m_new = jnp.maximum(m_old, jnp.max(s, axis=0, keepdims=True))
            alpha = jnp.exp2(m_old - m_new)
            p = jnp.exp2(s - m_new)
            m_ref[...] = m_new
            l_ref[...] = alpha * l_ref[...] + jnp.sum(p, axis=0, keepdims=True)
            acc_ref[...] = alpha * acc_ref[...] + _dot(vt_ref[j], p.astype(BF16))
            return carry

        lax.fori_loop(0, n_kv, tile_step, 0)


    o = acc_ref[...] / l_ref[...]
    lam = (jnp.exp(jnp.sum(lq1_ref[...] * lk1_ref[...]))
           - jnp.exp(jnp.sum(lq2_ref[...] * lk2_ref[...])) + lam_init)
    o = o[:, :tq] - lam * o[:, tq:]
    o = o * lax.rsqrt(jnp.mean(o * o, axis=0, keepdims=True) + EPS) * g_ref[...]
    o_ref[...] = (o * (1.0 - lam_init)).T.astype(o_ref.dtype)


def _diff(q, k, vt, lq1, lk1, lq2, lk2, subln_g, lam_init):
    b, s, w = q.shape
    tq = DIFF_Q_TILE
    _, heads, n_kv, dv, kv_tile = vt.shape
    q_spec = pl.BlockSpec((None, tq, LANES), lambda bi, h, i: (bi, i, h))
    k_spec = pl.BlockSpec((None, s, LANES), lambda bi, h, i: (bi, 0, h))
    vt_spec = pl.BlockSpec((None, None, n_kv, dv, kv_tile), lambda bi, h, i: (bi, h, 0, 0, 0))
    vec = lambda n: pl.BlockSpec((1, n), lambda bi, h, i: (0, 0))
    return pl.pallas_call(
        functools.partial(_diff_kernel, lam_init=lam_init),
        grid=(b, heads, s // tq),
        in_specs=[q_spec, k_spec, vt_spec, vec(HEAD_DIM), vec(HEAD_DIM), vec(HEAD_DIM),
                  vec(HEAD_DIM), pl.BlockSpec((dv, 1), lambda bi, h, i: (0, 0))],
        out_specs=q_spec,
        out_shape=jax.ShapeDtypeStruct((b, s, w), BF16),
        scratch_shapes=[pltpu.VMEM((1, 2 * tq), F32), pltpu.VMEM((1, 2 * tq), F32),
                        pltpu.VMEM((dv, 2 * tq), F32),
                        pltpu.VMEM((kv_tile, 2 * tq), BF16), pltpu.VMEM((kv_tile, 2 * tq), BF16),
                        pltpu.VMEM((2 * tq, LANES), BF16)],
        compiler_params=pltpu.CompilerParams(
            dimension_semantics=("arbitrary", "arbitrary", "arbitrary"),
            vmem_limit_bytes=32 * MIB),
        name="differential_attn",
    )(q, k, vt, lq1, lk1, lq2, lk2, subln_g)


def _merge_kernel(h_ref, u_ref, oa_ref, ob_ref, oc_ref, wg_ref, bg_ref, wb_ref, wo_ref, g_ref,
                  o_ref):
    d = h_ref.shape[1]
    u = u_ref[...]
    merged = None
    for n, br_ref in enumerate((oa_ref, ob_ref, oc_ref)):
        gate = jax.nn.sigmoid(_dot(u, wg_ref[:, n * d:(n + 1) * d]) + bg_ref[n:n + 1, :])
        term = gate * _dot(br_ref[...], wb_ref[n])
        merged = term if merged is None else merged + term
    y = _dot(merged.astype(BF16), wo_ref[...])
    o_ref[...] = h_ref[...] + _rmsnorm(y, g_ref[...])


def _merge(h, u, oa, ob, oc, w_gate, b_gate, w_branch, w_out, post_g):
    t, d = h.shape
    rows = lambda w: pl.BlockSpec((TOKEN_TILE, w), lambda i: (i, 0))
    return pl.pallas_call(
        _merge_kernel,
        grid=(t // TOKEN_TILE,),
        in_specs=[rows(d), rows(d), rows(oa.shape[1]), rows(ob.shape[1]), rows(oc.shape[1]),
                  _resident(w_gate.shape), _resident(b_gate.shape), _resident(w_branch.shape),
                  _resident(w_out.shape), _resident((1, d))],
        out_specs=rows(d),
        out_shape=jax.ShapeDtypeStruct((t, d), F32),
        compiler_params=pltpu.CompilerParams(
            dimension_semantics=("arbitrary",), vmem_limit_bytes=40 * MIB),
        name="gated_merge",
    )(h, u, oa, ob, oc, w_gate, b_gate, w_branch, w_out, post_g)


def _rope_tables(s):
    pos = jnp.arange(s, dtype=F32)
    inv = ROPE_THETA ** (-jnp.arange(0, HEAD_DIM, 2, dtype=F32) / HEAD_DIM)
    ang = pos[:, None] * inv[None, :]
    ang = jnp.concatenate([ang, ang], axis=-1)
    cos = jnp.tile(jnp.cos(ang), (1, LANES // HEAD_DIM))
    sin = jnp.tile(jnp.sin(ang), (1, LANES // HEAD_DIM))
    first_half = (jnp.arange(LANES) % HEAD_DIM) < HEAD_DIM // 2
    return cos, jnp.where(first_half, -sin, 0.0), jnp.where(first_half, 0.0, sin)


def kernel(x, ffn1_pre_g, ffn1_w_gu, ffn1_w_down, ffn1_post_g, mix_pre_g, w_in, na_rpb, sw_sink, diff_lambda_q1, diff_lambda_k1, diff_lambda_q2, diff_lambda_k2, diff_subln_g, w_branch, b_gate, w_out, mix_post_g, ffn2_pre_g, ffn2_w_gu, ffn2_w_down, ffn2_post_g):
    b, s, d = x.shape
    depth = w_in.shape[0]
    assert s % (NA_ROWS_PER_STEP * GRID_W) == 0 and s % TOKEN_TILE == 0 and s >= NA_ROW_K * GRID_W
    cos, sin_lo, sin_hi = _rope_tables(s)
    row = lambda a: a.reshape(1, -1).astype(F32)
    h = x.reshape(b * s, d)
    for l in range(depth):
        h = _ffn(h, row(ffn1_pre_g[l]), ffn1_w_gu[l].astype(BF16), ffn1_w_down[l].astype(BF16),
                 row(ffn1_post_g[l]))

        w_l = w_in[l].astype(BF16)
        u, qa, ka, va, qb, kb, vb, qc, kc, vc = _proj(
            h, row(mix_pre_g[l]), w_l[:, :QKV_COLS], cos, sin_lo, sin_hi)
        seq = lambda a: a.reshape(b, s, a.shape[-1])
        oa = _na(seq(qa), seq(ka), seq(va).transpose(0, 2, 1),
                 _na_bias_table(na_rpb[l], s // GRID_W))
        ob = _sw(seq(qb), seq(kb), seq(vb).transpose(0, 2, 1),
                 sw_sink[l].reshape(1, -1).astype(F32))
        lam_init = 0.8 - 0.6 * math.exp(-0.3 * l)
        kv_tile = min(DIFF_KV_TILE, s)
        vct = vc.reshape(b, s // kv_tile, kv_tile, DIFF_HEADS, 2 * HEAD_DIM).transpose(0, 3, 1, 4, 2)
        oc = _diff(seq(qc), seq(kc), vct, row(diff_lambda_q1[l]), row(diff_lambda_k1[l]),
                   row(diff_lambda_q2[l]), row(diff_lambda_k2[l]),
                   diff_subln_g[l].reshape(-1, 1).astype(F32), lam_init)
        flat = lambda a: a.reshape(b * s, a.shape[-1])
        h = _merge(h, u, flat(oa), flat(ob), flat(oc), w_l[:, QKV_COLS:], b_gate[l].astype(F32),
                   w_branch[l].astype(BF16), w_out[l].astype(BF16), row(mix_post_g[l]))

        h = _ffn(h, row(ffn2_pre_g[l]), ffn2_w_gu[l].astype(BF16), ffn2_w_down[l].astype(BF16),
                 row(ffn2_post_g[l]))
    return h.reshape(b, s, d)
```

```python
import functools
import math

import jax
import jax.numpy as jnp
from jax import lax
from jax.experimental import pallas as pl
from jax.experimental.pallas import tpu as pltpu

F32 = jnp.float32
BF16 = jnp.bfloat16

HEAD_DIM = 64
GRID_W = 64
NA_HEADS = 8
NA_ROW_K = 8
NA_COL_K = 16
SW_Q_HEADS = 8
SW_KV_HEADS = 2
SW_WINDOW = 128
DIFF_HEADS = 4
N_BRANCH = 3
ROPE_THETA = 10000.0
EPS = 1e-6
NEG = -1e30
LANES = 128
MIB = 1024 * 1024

A_W = NA_HEADS * HEAD_DIM
B_QW = SW_Q_HEADS * HEAD_DIM
B_KVW = SW_KV_HEADS * HEAD_DIM
C_W = 2 * DIFF_HEADS * HEAD_DIM
_SPLIT_EDGES = (0, A_W, 2 * A_W, 3 * A_W, 3 * A_W + B_QW, 3 * A_W + B_QW + B_KVW,
                3 * A_W + B_QW + 2 * B_KVW, 3 * A_W + B_QW + 2 * B_KVW + C_W,
                3 * A_W + B_QW + 2 * B_KVW + 2 * C_W, 3 * A_W + B_QW + 2 * B_KVW + 3 * C_W, None)

TOKEN_TILE = 512
NA_ROWS_PER_STEP = 4
SW_Q_TILE = 256
DIFF_Q_TILE = 256
DIFF_KV_TILE = 512
DIFF_SCORE_ROWS = 512
DIFF_PIECE_ROWS = 16
DIFF_WARM_ROWS = 128
DIFF_REF_SLACK = 40.0
SUBLANES = 8


def _rmsnorm(x, g):
    return x * lax.rsqrt(jnp.mean(x * x, axis=-1, keepdims=True) + EPS) * g


def _dot(a, b):
    return jnp.dot(a, b, preferred_element_type=F32)


def _dot_nt(a, b):
    return lax.dot_general(a, b, (((1,), (1,)), ((), ())), preferred_element_type=F32)


def _resident(shape):
    return pl.BlockSpec(shape, lambda *_: (0,) * len(shape), pipeline_mode=pl.Buffered(1))


def _low_half_mask():
    return lax.broadcasted_iota(jnp.int32, (1, LANES), 1) < HEAD_DIM


def _head_pipeline(heads, scores, attend):
    heads = list(heads)
    out = []
    s_next = scores(heads[0])
    for idx, head in enumerate(heads):
        s = s_next
        if idx + 1 < len(heads):
            s_next = scores(heads[idx + 1])
        out.append(attend(head, s))
    return out


def _ffn_kernel(x_ref, pre_g_ref, wgu_ref, wdown_ref, post_g_ref, o_ref, *, f_chunks, d_ff):
    x = x_ref[...]
    xn = _rmsnorm(x, pre_g_ref[...]).astype(BF16)
    acc = None
    for c0, c1 in f_chunks:
        gate = _dot(xn, wgu_ref[:, c0:c1])
        up = _dot(xn, wgu_ref[:, d_ff + c0:d_ff + c1])
        act = (gate * jax.nn.sigmoid(gate) * up).astype(BF16)
        part = _dot(act, wdown_ref[c0:c1, :])
        acc = part if acc is None else acc + part
    o_ref[...] = x + 0.5 * _rmsnorm(acc, post_g_ref[...])


def _ffn(h, pre_g, w_gu, w_down, post_g):
    t, d = h.shape
    d_ff = w_down.shape[0]
    step = 1024
    f_chunks = tuple((c, min(c + step, d_ff)) for c in range(0, d_ff, step))
    row = pl.BlockSpec((TOKEN_TILE, d), lambda i: (i, 0))
    return pl.pallas_call(
        functools.partial(_ffn_kernel, f_chunks=f_chunks, d_ff=d_ff),
        grid=(t // TOKEN_TILE,),
        in_specs=[row, _resident((1, d)), _resident(w_gu.shape), _resident(w_down.shape),
                  _resident((1, d))],
        out_specs=row,
        out_shape=jax.ShapeDtypeStruct((t, d), F32),
        compiler_params=pltpu.CompilerParams(
            dimension_semantics=("arbitrary",), vmem_limit_bytes=48 * MIB),
        name="ffn",
    )(h, pre_g, w_gu, w_down, post_g)


def _rope(t, cos, sin_lo, sin_hi):
    out = []
    for c in range(t.shape[1] // LANES):
        tc = t[:, c * LANES:(c + 1) * LANES]
        fwd = pltpu.roll(tc, LANES - HEAD_DIM // 2, axis=1)
        back = pltpu.roll(tc, HEAD_DIM // 2, axis=1)
        out.append(tc * cos + fwd * sin_lo + back * sin_hi)
    return jnp.concatenate(out, axis=1)


def _proj_kernel(h_ref, g_ref, wqk_ref, wvt_ref, cos_ref, sin_lo_ref, sin_hi_ref,
                 u_ref, qa_ref, ka_ref, qb_ref, kb_ref, qc_ref, kc_ref, vat_ref, vbt_ref, vct_ref):
    un = _rmsnorm(h_ref[...], g_ref[...]).astype(BF16)
    u_ref[...] = un
    cos, sin_lo, sin_hi = cos_ref[...], sin_lo_ref[...], sin_hi_ref[...]
    scale = HEAD_DIM ** -0.5
    col = 0

    def seg(width):
        nonlocal col
        out = _dot(un, wqk_ref[:, col:col + width])
        col += width
        return out

    qa_ref[...] = (seg(A_W) * scale).astype(BF16)
    ka_ref[...] = seg(A_W).astype(BF16)
    qb_ref[...] = (_rope(seg(B_QW), cos, sin_lo, sin_hi) * scale).astype(BF16)
    kb_ref[...] = _rope(seg(B_KVW), cos, sin_lo, sin_hi).astype(BF16)
    qc_ref[...] = (_rope(seg(C_W), cos, sin_lo, sin_hi) * (scale * math.log2(math.e))).astype(BF16)
    kc_ref[...] = _rope(seg(C_W), cos, sin_lo, sin_hi).astype(BF16)
    row = 0
    for vt_ref in (vat_ref, vbt_ref, vct_ref):
        width = vt_ref.shape[0]
        vt_ref[...] = _dot_nt(wvt_ref[row:row + width, :], un).astype(BF16)
        row += width


def _proj(h, g, w_qk, w_vt, cos, sin_lo, sin_hi):
    t, d = h.shape
    s = cos.shape[0]
    tiles_per_seq = s // TOKEN_TILE
    widths = (d, A_W, A_W, B_QW, B_KVW, C_W, C_W)
    v_widths = (A_W, B_KVW, C_W)
    rows = lambda w: pl.BlockSpec((TOKEN_TILE, w), lambda i: (i, 0))
    chans = lambda w: pl.BlockSpec((None, w, TOKEN_TILE),
                                   lambda i: (i // tiles_per_seq, 0, i % tiles_per_seq))
    table = pl.BlockSpec((TOKEN_TILE, LANES), lambda i: (i % tiles_per_seq, 0))
    return pl.pallas_call(
        _proj_kernel,
        grid=(t // TOKEN_TILE,),
        in_specs=[rows(d), _resident((1, d)), _resident(w_qk.shape), _resident(w_vt.shape),
                  table, table, table],
        out_specs=[rows(w) for w in widths] + [chans(w) for w in v_widths],
        out_shape=([jax.ShapeDtypeStruct((t, w), BF16) for w in widths]
                   + [jax.ShapeDtypeStruct((t // s, w, s), BF16) for w in v_widths]),
        compiler_params=pltpu.CompilerParams(
            dimension_semantics=("arbitrary",), vmem_limit_bytes=40 * MIB),
        name="proj",
    )(h, g, w_qk, w_vt, cos, sin_lo, sin_hi)


def _na_kernel(q_ref, kp_ref, kc_ref, kn_ref, vtp_ref, vtc_ref, vtn_ref, bias_ref, o_ref):
    blk = q_ref.shape[0]
    k_refs = (kp_ref, kc_ref, kn_ref)
    vt_refs = (vtp_ref, vtc_ref, vtn_ref)
    low = _low_half_mask()
    top = lax.broadcasted_iota(jnp.int32, (LANES, 1), 0) < HEAD_DIM
    lanes = lambda head: slice(head // 2 * LANES, (head // 2 + 1) * LANES)

    def scores(head):
        qc = q_ref[:, lanes(head)]
        qz = jnp.where(low if head % 2 == 0 else ~low, qc, jnp.zeros_like(qc))
        return [_dot_nt(k_refs[w][:, lanes(head)], qz) + bias_ref[head, w * blk:(w + 1) * blk, :]
                for w in range(3)]

    def attend(head, s):
        m = jnp.max(jnp.maximum(jnp.maximum(s[0], s[1]), s[2]), axis=0, keepdims=True)
        p = [jnp.exp(sw - m) for sw in s]
        l = jnp.sum(p[0] + p[1] + p[2], axis=0, keepdims=True)
        o = sum(_dot(vt_refs[w][lanes(head), :], p[w].astype(BF16)) for w in range(3))
        return o / l

    res = _head_pipeline(range(NA_HEADS), scores, attend)
    for c in range(A_W // LANES):
        o_ref[:, c * LANES:(c + 1) * LANES] = (
            jnp.where(top, res[2 * c], res[2 * c + 1]).T.astype(o_ref.dtype))


def _na_bias_table(rpb, grid_rows):
    n = NA_ROWS_PER_STEP
    c = jnp.arange(GRID_W)
    dc = jnp.clip(c[:, None] - c[None, :] + (NA_COL_K - 1), 0, 2 * NA_COL_K - 2)
    col_start = jnp.clip(c - NA_COL_K // 2, 0, GRID_W - NA_COL_K)
    col_ok = (c[:, None] >= col_start[None, :]) & (c[:, None] < col_start[None, :] + NA_COL_K)
    kr = jnp.arange(3 * n)[:, None]
    qr = jnp.arange(n)[None, :]
    dr = kr - qr + (NA_ROW_K - 1 - n)
    tables = []
    for first_row in (0, n, grid_rows - n):
        q_abs = first_row + qr
        k_abs = first_row - n + kr
        row_start = jnp.clip(q_abs - NA_ROW_K // 2, 0, grid_rows - NA_ROW_K)
        row_ok = (k_abs >= row_start) & (k_abs < row_start + NA_ROW_K)
        vals = rpb.astype(F32)[:, jnp.clip(dr, 0, 2 * NA_ROW_K - 2)][:, :, :, dc]
        ok = row_ok[:, :, None, None] & col_ok[None, None, :, :]
        t = jnp.where(ok[None], vals, NEG).transpose(0, 1, 3, 2, 4)
        tables.append(t.reshape(rpb.shape[0], 3 * n * GRID_W, n * GRID_W))
    return jnp.stack(tables)


def _na(q, k, vt, bias):
    b, s, w = q.shape
    blk = NA_ROWS_PER_STEP * GRID_W
    n_blk = s // blk
    clamp = lambda j: jnp.clip(j, 0, n_blk - 1)
    tok = lambda shift: pl.BlockSpec((None, blk, w), lambda bi, j: (bi, clamp(j + shift), 0))
    chan = lambda shift: pl.BlockSpec((None, w, blk), lambda bi, j: (bi, 0, clamp(j + shift)))
    table = pl.BlockSpec(
        (None,) + bias.shape[1:],
        lambda bi, j: (jnp.where(j == 0, 0, jnp.where(j == n_blk - 1, 2, 1)), 0, 0, 0))
    return pl.pallas_call(
        _na_kernel,
        grid=(b, n_blk),
        in_specs=[tok(0), tok(-1), tok(0), tok(1), chan(-1), chan(0), chan(1), table],
        out_specs=tok(0),
        out_shape=jax.ShapeDtypeStruct((b, s, w), BF16),
        compiler_params=pltpu.CompilerParams(
            dimension_semantics=("arbitrary", "arbitrary"), vmem_limit_bytes=40 * MIB),
        name="neighbourhood_attn",
    )(q, k, k, k, vt, vt, vt, bias)


def _sw_kernel(sink_ref, q_ref, kp_ref, kc_ref, kn_ref, vp_ref, vc_ref, vn_ref, o_ref, *, seq):
    i = pl.program_id(1)
    tq = q_ref.shape[0]
    k = jnp.concatenate([kp_ref[...], kc_ref[...], kn_ref[...]], axis=0)
    vt = jnp.concatenate([vp_ref[...], vc_ref[...], vn_ref[...]], axis=1)
    nk = k.shape[0]
    kpos = i * tq - SW_WINDOW + lax.broadcasted_iota(jnp.int32, (nk, tq), 0)
    qpos = i * tq + lax.broadcasted_iota(jnp.int32, (nk, tq), 1)
    ok = (jnp.abs(qpos - kpos) <= SW_WINDOW) & (kpos >= 0) & (kpos < seq)
    low = _low_half_mask()
    group = SW_Q_HEADS // SW_KV_HEADS

    def scores(head):
        kv_head = head // group
        qc = q_ref[:, head // 2 * LANES:(head // 2 + 1) * LANES]
        src = qc if head % 2 == kv_head else pltpu.roll(qc, HEAD_DIM, axis=1)
        qz = jnp.where(low if kv_head == 0 else ~low, src, jnp.zeros_like(src))
        return jnp.where(ok, _dot_nt(k, qz), NEG)

    def attend(head, s):
        kv_head = head // group
        sink = sink_ref[0, head]
        m = jnp.maximum(jnp.max(s, axis=0, keepdims=True), sink)
        p = jnp.exp(s - m)
        l = jnp.sum(p, axis=0, keepdims=True) + jnp.exp(sink - m)
        o = _dot(vt, p.astype(BF16)) / l
        return o[kv_head * HEAD_DIM:(kv_head + 1) * HEAD_DIM]

    res = _head_pipeline(range(SW_Q_HEADS), scores, attend)
    for c in range(B_QW // LANES):
        o_ref[:, c * LANES:(c + 1) * LANES] = (
            jnp.concatenate(res[2 * c:2 * c + 2], axis=0).T.astype(o_ref.dtype))


def _sw(q, k, vt, sink):
    b, s, w = q.shape
    kvw = k.shape[-1]
    tq = SW_Q_TILE
    per = tq // SW_WINDOW
    n_blk = s // SW_WINDOW
    prev = lambda i: jnp.maximum(per * i - 1, 0)
    nxt = lambda i: jnp.minimum(per * (i + 1), n_blk - 1)
    q_spec = pl.BlockSpec((None, tq, w), lambda bi, i: (bi, i, 0))
    k_specs = [pl.BlockSpec((None, SW_WINDOW, kvw), lambda bi, i: (bi, prev(i), 0)),
               pl.BlockSpec((None, tq, kvw), lambda bi, i: (bi, i, 0)),
               pl.BlockSpec((None, SW_WINDOW, kvw), lambda bi, i: (bi, nxt(i), 0))]
    vt_specs = [pl.BlockSpec((None, kvw, SW_WINDOW), lambda bi, i: (bi, 0, prev(i))),
                pl.BlockSpec((None, kvw, tq), lambda bi, i: (bi, 0, i)),
                pl.BlockSpec((None, kvw, SW_WINDOW), lambda bi, i: (bi, 0, nxt(i)))]
    sink_spec = pl.BlockSpec(memory_space=pltpu.SMEM)
    return pl.pallas_call(
        functools.partial(_sw_kernel, seq=s),
        grid=(b, s // tq),
        in_specs=[sink_spec, q_spec] + k_specs + vt_specs,
        out_specs=q_spec,
        out_shape=jax.ShapeDtypeStruct((b, s, w), BF16),
        compiler_params=pltpu.CompilerParams(dimension_semantics=("arbitrary", "arbitrary")),
        name="sliding_window_attn",
    )(sink, q, k, k, k, vt, vt, vt)


def _diff_kernel(q_ref, k_ref, vt_ref, lq1_ref, lk1_ref, lq2_ref, lk2_ref, g_ref, o_ref,
                 m_ref, l_ref, acc_ref, p0_ref, p1_ref, qz_ref, *, lam_init):
    tq = q_ref.shape[0]
    kv_tile = p0_ref.shape[0]
    n_kv = k_ref.shape[0] // kv_tile
    cols = 2 * tq
    p_bufs = (p0_ref, p1_ref)
    low = _low_half_mask()
    q = q_ref[...]
    zero = jnp.zeros_like(q)
    qz_ref[0:tq, :] = jnp.where(low, q, zero)
    qz_ref[tq:, :] = jnp.where(low, zero, q)

    def fold8(x, op):
        return op(x.reshape(x.shape[0] // SUBLANES, SUBLANES, x.shape[1]), axis=0)

    def col_max(x8):
        return jnp.max(x8, axis=0, keepdims=True)

    lam = (jnp.exp(jnp.sum(lq1_ref[...] * lk1_ref[...]))
           - jnp.exp(jnp.sum(lq2_ref[...] * lk2_ref[...])) + lam_init)

    def finalize():
        o = acc_ref[...] / l_ref[...]
        o = o[:, :tq] - lam * o[:, tq:]
        o = o * lax.rsqrt(jnp.mean(o * o, axis=0, keepdims=True) + EPS) * g_ref[...]
        o_ref[...] = (o * (1.0 - lam_init)).T.astype(o_ref.dtype)

    ref = None
    unsafe = jnp.zeros((1, cols), F32)
    l8 = None
    beta = None
    pending = None

    def value_product(tile, scale):
        pv = _dot(vt_ref[:, tile * kv_tile:(tile + 1) * kv_tile], p_bufs[tile % 2][...])
        acc_ref[...] = pv if scale is None else scale * acc_ref[...] + pv

    for t in range(n_kv):
        tile_max8 = None
        tile_sum8 = None
        n_chunks = kv_tile // DIFF_SCORE_ROWS
        for c in range(n_chunks):
            start = t * kv_tile + c * DIFF_SCORE_ROWS
            s = _dot_nt(k_ref[start:start + DIFF_SCORE_ROWS, :], qz_ref[...])
            if ref is None:
                ref = col_max(fold8(s[0:DIFF_WARM_ROWS], jnp.max))
            if c == n_chunks // 2 and pending is not None:
                value_product(*pending)
                pending = None
            for r in range(0, DIFF_SCORE_ROWS, DIFF_PIECE_ROWS):
                piece = s[r:r + DIFF_PIECE_ROWS]
                mx = fold8(piece, jnp.max)
                tile_max8 = mx if tile_max8 is None else jnp.maximum(tile_max8, mx)
                p = jnp.exp2(piece - ref)
                ps = fold8(p, jnp.sum)
                tile_sum8 = ps if tile_sum8 is None else tile_sum8 + ps
                row = c * DIFF_SCORE_ROWS + r
                p_bufs[t % 2][row:row + DIFF_PIECE_ROWS, :] = p.astype(BF16)
        if pending is not None:
            value_product(*pending)
        pending = (t, beta)
        l8 = tile_sum8 if beta is None else beta * l8 + tile_sum8
        tile_max = col_max(tile_max8)
        unsafe = jnp.maximum(unsafe, jnp.where(tile_max - ref > DIFF_REF_SLACK, 1.0, 0.0))
        new_ref = jnp.maximum(ref, tile_max)
        beta = jnp.exp2(ref - new_ref)
        ref = new_ref
    value_product(*pending)
    l_fast = jnp.sum(l8, axis=0, keepdims=True)
    l_ref[...] = l_fast
    finite = jnp.isfinite(jnp.sum(acc_ref[...], axis=0, keepdims=True) + l_fast)
    unsafe = jnp.maximum(unsafe, jnp.where(finite, 0.0, 1.0))
    finalize()

    @pl.when(jnp.max(unsafe) > 0.0)
    def _():
        m_ref[...] = jnp.full(m_ref.shape, -jnp.inf, F32)
        l_ref[...] = jnp.zeros(l_ref.shape, F32)
        acc_ref[...] = jnp.zeros(acc_ref.shape, F32)

        def tile_step(j, carry):
            rows = pl.ds(pl.multiple_of(j * kv_tile, kv_tile), kv_tile)
            s = _dot_nt(k_ref[rows, :], qz_ref[...])
            m_old = m_ref[...]
            m_new = jnp.maximum(m_old, jnp.max(s, axis=0, keepdims=True))
            alpha = jnp.exp2(m_old - m_new)
            p = jnp.exp2(s - m_new)
            m_ref[...] = m_new
            l_ref[...] = alpha * l_ref[...] + jnp.sum(p, axis=0, keepdims=True)
            acc_ref[...] = alpha * acc_ref[...] + _dot(vt_ref[:, rows], p.astype(BF16))
            return carry

        lax.fori_loop(0, n_kv, tile_step, 0)
        finalize()


def _diff(q, k, vt, lq1, lk1, lq2, lk2, subln_g, lam_init):
    b, s, w = q.shape
    tq = DIFF_Q_TILE
    dv = 2 * HEAD_DIM
    heads = w // LANES
    kv_tile = min(DIFF_KV_TILE, s)
    q_spec = pl.BlockSpec((None, tq, LANES), lambda bi, h, i: (bi, i, h))
    k_spec = pl.BlockSpec((None, s, LANES), lambda bi, h, i: (bi, 0, h))
    vt_spec = pl.BlockSpec((None, dv, s), lambda bi, h, i: (bi, h, 0))
    vec = lambda n: pl.BlockSpec((1, n), lambda bi, h, i: (0, 0))
    return pl.pallas_call(
        functools.partial(_diff_kernel, lam_init=lam_init),
        grid=(b, heads, s // tq),
        in_specs=[q_spec, k_spec, vt_spec, vec(HEAD_DIM), vec(HEAD_DIM), vec(HEAD_DIM),
                  vec(HEAD_DIM), pl.BlockSpec((dv, 1), lambda bi, h, i: (0, 0))],
        out_specs=q_spec,
        out_shape=jax.ShapeDtypeStruct((b, s, w), BF16),
        scratch_shapes=[pltpu.VMEM((1, 2 * tq), F32), pltpu.VMEM((1, 2 * tq), F32),
                        pltpu.VMEM((dv, 2 * tq), F32),
                        pltpu.VMEM((kv_tile, 2 * tq), BF16), pltpu.VMEM((kv_tile, 2 * tq), BF16),
                        pltpu.VMEM((2 * tq, LANES), BF16)],
        compiler_params=pltpu.CompilerParams(
            dimension_semantics=("arbitrary", "arbitrary", "arbitrary"),
            vmem_limit_bytes=32 * MIB),
        name="differential_attn",
    )(q, k, vt, lq1, lk1, lq2, lk2, subln_g)


def _merge_kernel(h_ref, u_ref, oa_ref, ob_ref, oc_ref, wg_ref, bg_ref, wb_ref, wo_ref, g_ref,
                  o_ref):
    d = h_ref.shape[1]
    u = u_ref[...]
    merged = None
    for n, br_ref in enumerate((oa_ref, ob_ref, oc_ref)):
        gate = jax.nn.sigmoid(_dot(u, wg_ref[:, n * d:(n + 1) * d]) + bg_ref[n:n + 1, :])
        term = gate * _dot(br_ref[...], wb_ref[n])
        merged = term if merged is None else merged + term
    y = _dot(merged.astype(BF16), wo_ref[...])
    o_ref[...] = h_ref[...] + _rmsnorm(y, g_ref[...])


def _merge(h, u, oa, ob, oc, w_gate, b_gate, w_branch, w_out, post_g):
    t, d = h.shape
    rows = lambda w: pl.BlockSpec((TOKEN_TILE, w), lambda i: (i, 0))
    return pl.pallas_call(
        _merge_kernel,
        grid=(t // TOKEN_TILE,),
        in_specs=[rows(d), rows(d), rows(oa.shape[1]), rows(ob.shape[1]), rows(oc.shape[1]),
                  _resident(w_gate.shape), _resident(b_gate.shape), _resident(w_branch.shape),
                  _resident(w_out.shape), _resident((1, d))],
        out_specs=rows(d),
        out_shape=jax.ShapeDtypeStruct((t, d), F32),
        compiler_params=pltpu.CompilerParams(
            dimension_semantics=("arbitrary",), vmem_limit_bytes=40 * MIB),
        name="gated_merge",
    )(h, u, oa, ob, oc, w_gate, b_gate, w_branch, w_out, post_g)


def _rope_tables(s):
    pos = jnp.arange(s, dtype=F32)
    inv = ROPE_THETA ** (-jnp.arange(0, HEAD_DIM, 2, dtype=F32) / HEAD_DIM)
    ang = pos[:, None] * inv[None, :]
    ang = jnp.concatenate([ang, ang], axis=-1)
    cos = jnp.tile(jnp.cos(ang), (1, LANES // HEAD_DIM))
    sin = jnp.tile(jnp.sin(ang), (1, LANES // HEAD_DIM))
    first_half = (jnp.arange(LANES) % HEAD_DIM) < HEAD_DIM // 2
    return cos, jnp.where(first_half, -sin, 0.0), jnp.where(first_half, 0.0, sin)


def kernel(x, ffn1_pre_g, ffn1_w_gu, ffn1_w_down, ffn1_post_g, mix_pre_g, w_in, na_rpb, sw_sink, diff_lambda_q1, diff_lambda_k1, diff_lambda_q2, diff_lambda_k2, diff_subln_g, w_branch, b_gate, w_out, mix_post_g, ffn2_pre_g, ffn2_w_gu, ffn2_w_down, ffn2_post_g):
    b, s, d = x.shape
    depth = w_in.shape[0]
    assert s % (NA_ROWS_PER_STEP * GRID_W) == 0 and s % TOKEN_TILE == 0 and s >= NA_ROW_K * GRID_W
    cos, sin_lo, sin_hi = _rope_tables(s)
    row = lambda a: a.reshape(1, -1).astype(F32)
    h = x.reshape(b * s, d)
    for l in range(depth):
        h = _ffn(h, row(ffn1_pre_g[l]), ffn1_w_gu[l].astype(BF16), ffn1_w_down[l].astype(BF16),
                 row(ffn1_post_g[l]))

        w_l = w_in[l]
        qa_w, ka_w, va_w, qb_w, kb_w, vb_w, qc_w, kc_w, vc_w, gate_w = (
            w_l[:, c0:c1] for c0, c1 in zip(_SPLIT_EDGES[:-1], _SPLIT_EDGES[1:]))
        w_qk = jnp.concatenate([qa_w, ka_w, qb_w, kb_w, qc_w, kc_w], axis=1).astype(BF16)
        w_vt = jnp.concatenate([va_w, vb_w, vc_w], axis=1).T.astype(BF16)
        u, qa, ka, qb, kb, qc, kc, vat, vbt, vct = _proj(
            h, row(mix_pre_g[l]), w_qk, w_vt, cos, sin_lo, sin_hi)
        seq = lambda a: a.reshape(b, s, a.shape[-1])
        oa = _na(seq(qa), seq(ka), vat, _na_bias_table(na_rpb[l], s // GRID_W))
        ob = _sw(seq(qb), seq(kb), vbt, sw_sink[l].reshape(1, -1).astype(F32))
        lam_init = 0.8 - 0.6 * math.exp(-0.3 * l)
        oc = _diff(seq(qc), seq(kc), vct, row(diff_lambda_q1[l]), row(diff_lambda_k1[l]),
                   row(diff_lambda_q2[l]), row(diff_lambda_k2[l]),
                   diff_subln_g[l].reshape(-1, 1).astype(F32), lam_init)
        flat = lambda a: a.reshape(b * s, a.shape[-1])
        h = _merge(h, u, flat(oa), flat(ob), flat(oc), gate_w.astype(BF16), b_gate[l].astype(F32),
                   w_branch[l].astype(BF16), w_out[l].astype(BF16), row(mix_post_g[l]))

        h = _ffn(h, row(ffn2_pre_g[l]), ffn2_w_gu[l].astype(BF16), ffn2_w_down[l].astype(BF16),
                 row(ffn2_post_g[l]))
    return h.reshape(b, s, d)
```

```python
import functools
import math

import jax
import jax.numpy as jnp
import numpy as np
from jax import lax
from jax.experimental import pallas as pl
from jax.experimental.pallas import tpu as pltpu

F32 = jnp.float32
BF16 = jnp.bfloat16

HEAD_DIM = 64
GRID_W = 64
NA_HEADS = 8
NA_ROW_K = 8
NA_COL_K = 16
SW_Q_HEADS = 8
SW_KV_HEADS = 2
SW_WINDOW = 128
DIFF_HEADS = 4
N_BRANCH = 3
ROPE_THETA = 10000.0
EPS = 1e-6
LOG2E = math.log2(math.e)
NEG = -1e30
LANES = 128
MIB = 1024 * 1024

A_W = NA_HEADS * HEAD_DIM
B_QW = SW_Q_HEADS * HEAD_DIM
B_KVW = SW_KV_HEADS * HEAD_DIM
C_W = 2 * DIFF_HEADS * HEAD_DIM
_SPLIT_EDGES = (0, A_W, 2 * A_W, 3 * A_W, 3 * A_W + B_QW, 3 * A_W + B_QW + B_KVW,
                3 * A_W + B_QW + 2 * B_KVW, 3 * A_W + B_QW + 2 * B_KVW + C_W,
                3 * A_W + B_QW + 2 * B_KVW + 2 * C_W, 3 * A_W + B_QW + 2 * B_KVW + 3 * C_W, None)

TOKEN_TILE = 512
NA_ROWS_PER_STEP = 4
SW_Q_TILE = 256
DIFF_Q_TILE = 256
DIFF_KV_TILE = 512
DIFF_Q_SUBTILES = 2
DIFF_PIECE_ROWS = 16
DIFF_WARM_ROWS = 128
DIFF_REF_SLACK = 40.0
SUBLANES = 8
HEAD_PIPELINE_DEPTH = 4


def _rmsnorm(x, g):
    return x * lax.rsqrt(jnp.mean(x * x, axis=-1, keepdims=True) + EPS) * g


def _dot(a, b):
    return jnp.dot(a, b, preferred_element_type=F32)


def _dot_nt(a, b):
    return lax.dot_general(a, b, (((1,), (1,)), ((), ())), preferred_element_type=F32)


def _resident(shape):
    return pl.BlockSpec(shape, lambda *_: (0,) * len(shape), pipeline_mode=pl.Buffered(1))


def _low_half_mask():
    return lax.broadcasted_iota(jnp.int32, (1, LANES), 1) < HEAD_DIM


def _head_pipeline(heads, scores, attend, depth=HEAD_PIPELINE_DEPTH):
    heads = list(heads)
    ready = [scores(h) for h in heads[:depth]]
    out = []
    for idx, head in enumerate(heads):
        if idx + depth < len(heads):
            ready.append(scores(heads[idx + depth]))
        out.append(attend(head, ready.pop(0)))
    return out


def _ffn_kernel(x_ref, pre_g_ref, wgu_ref, wdown_ref, post_g_ref, o_ref, *, f_chunks, d_ff):
    x = x_ref[...]
    xn = _rmsnorm(x, pre_g_ref[...]).astype(BF16)
    acc = None
    for c0, c1 in f_chunks:
        gate = _dot(xn, wgu_ref[:, c0:c1])
        up = _dot(xn, wgu_ref[:, d_ff + c0:d_ff + c1])
        act = (gate * jax.nn.sigmoid(gate) * up).astype(BF16)
        part = _dot(act, wdown_ref[c0:c1, :])
        acc = part if acc is None else acc + part
    o_ref[...] = x + 0.5 * _rmsnorm(acc, post_g_ref[...])


def _ffn(h, pre_g, w_gu, w_down, post_g):
    t, d = h.shape
    d_ff = w_down.shape[0]
    step = 1024
    f_chunks = tuple((c, min(c + step, d_ff)) for c in range(0, d_ff, step))
    row = pl.BlockSpec((TOKEN_TILE, d), lambda i: (i, 0))
    return pl.pallas_call(
        functools.partial(_ffn_kernel, f_chunks=f_chunks, d_ff=d_ff),
        grid=(t // TOKEN_TILE,),
        in_specs=[row, _resident((1, d)), _resident(w_gu.shape), _resident(w_down.shape),
                  _resident((1, d))],
        out_specs=row,
        out_shape=jax.ShapeDtypeStruct((t, d), F32),
        compiler_params=pltpu.CompilerParams(
            dimension_semantics=("arbitrary",), vmem_limit_bytes=48 * MIB),
        name="ffn",
    )(h, pre_g, w_gu, w_down, post_g)


def _rope(t, cos, sin_lo, sin_hi):
    out = []
    for c in range(t.shape[1] // LANES):
        tc = t[:, c * LANES:(c + 1) * LANES]
        fwd = pltpu.roll(tc, LANES - HEAD_DIM // 2, axis=1)
        back = pltpu.roll(tc, HEAD_DIM // 2, axis=1)
        out.append(tc * cos + fwd * sin_lo + back * sin_hi)
    return jnp.concatenate(out, axis=1)


def _proj_kernel(h_ref, g_ref, wqk_ref, wvt_ref, cos_ref, sin_lo_ref, sin_hi_ref,
                 u_ref, qa_ref, ka_ref, qb_ref, kb_ref, qc_ref, kc_ref, vat_ref, vbt_ref, vct_ref):
    un = _rmsnorm(h_ref[...], g_ref[...]).astype(BF16)
    u_ref[...] = un
    cos, sin_lo, sin_hi = cos_ref[...], sin_lo_ref[...], sin_hi_ref[...]
    scale = HEAD_DIM ** -0.5 * LOG2E
    col = 0

    def seg(width):
        nonlocal col
        out = _dot(un, wqk_ref[:, col:col + width])
        col += width
        return out

    qa_ref[...] = (seg(A_W) * scale).astype(BF16)
    ka_ref[...] = seg(A_W).astype(BF16)
    qb_ref[...] = (_rope(seg(B_QW), cos, sin_lo, sin_hi) * scale).astype(BF16)
    kb_ref[...] = _rope(seg(B_KVW), cos, sin_lo, sin_hi).astype(BF16)
    qc_ref[...] = (_rope(seg(C_W), cos, sin_lo, sin_hi) * scale).astype(BF16)
    kc_ref[...] = _rope(seg(C_W), cos, sin_lo, sin_hi).astype(BF16)
    row = 0
    for vt_ref in (vat_ref, vbt_ref, vct_ref):
        width = vt_ref.shape[0]
        vt_ref[...] = _dot_nt(wvt_ref[row:row + width, :], un).astype(BF16)
        row += width


def _proj(h, g, w_qk, w_vt, cos, sin_lo, sin_hi):
    t, d = h.shape
    s = cos.shape[0]
    tiles_per_seq = s // TOKEN_TILE
    widths = (d, A_W, A_W, B_QW, B_KVW, C_W, C_W)
    v_widths = (A_W, B_KVW, C_W)
    rows = lambda w: pl.BlockSpec((TOKEN_TILE, w), lambda i: (i, 0))
    chans = lambda w: pl.BlockSpec((None, w, TOKEN_TILE),
                                   lambda i: (i // tiles_per_seq, 0, i % tiles_per_seq))
    table = pl.BlockSpec((TOKEN_TILE, LANES), lambda i: (i % tiles_per_seq, 0))
    return pl.pallas_call(
        _proj_kernel,
        grid=(t // TOKEN_TILE,),
        in_specs=[rows(d), _resident((1, d)), _resident(w_qk.shape), _resident(w_vt.shape),
                  table, table, table],
        out_specs=[rows(w) for w in widths] + [chans(w) for w in v_widths],
        out_shape=([jax.ShapeDtypeStruct((t, w), BF16) for w in widths]
                   + [jax.ShapeDtypeStruct((t // s, w, s), BF16) for w in v_widths]),
        compiler_params=pltpu.CompilerParams(
            dimension_semantics=("arbitrary",), vmem_limit_bytes=40 * MIB),
        name="proj",
    )(h, g, w_qk, w_vt, cos, sin_lo, sin_hi)


def _na_kernel(q_ref, kp_ref, kc_ref, kn_ref, vtp_ref, vtc_ref, vtn_ref, bias_ref, o_ref):
    blk = q_ref.shape[0]
    k_refs = (kp_ref, kc_ref, kn_ref)
    vt_refs = (vtp_ref, vtc_ref, vtn_ref)
    low = _low_half_mask()
    top = lax.broadcasted_iota(jnp.int32, (LANES, 1), 0) < HEAD_DIM
    lanes = lambda head: slice(head // 2 * LANES, (head // 2 + 1) * LANES)

    def scores(head):
        qc = q_ref[:, lanes(head)]
        qz = jnp.where(low if head % 2 == 0 else ~low, qc, jnp.zeros_like(qc))
        return [_dot_nt(k_refs[w][:, lanes(head)], qz) + bias_ref[head, w * blk:(w + 1) * blk, :]
                for w in range(3)]

    def attend(head, s):
        m = jnp.max(jnp.maximum(jnp.maximum(s[0], s[1]), s[2]), axis=0, keepdims=True)
        p = [jnp.exp2(sw - m) for sw in s]
        l = jnp.sum(p[0] + p[1] + p[2], axis=0, keepdims=True)
        o = sum(_dot(vt_refs[w][lanes(head), :], p[w].astype(BF16)) for w in range(3))
        return o / l

    res = _head_pipeline(range(NA_HEADS), scores, attend)
    for c in range(A_W // LANES):
        o_ref[:, c * LANES:(c + 1) * LANES] = (
            jnp.where(top, res[2 * c], res[2 * c + 1]).T.astype(o_ref.dtype))


def _na_bias_table(rpb, grid_rows):
    n = NA_ROWS_PER_STEP
    c = np.arange(GRID_W)
    dc = np.clip(c[:, None] - c[None, :] + (NA_COL_K - 1), 0, 2 * NA_COL_K - 2)
    col_start = np.clip(c - NA_COL_K // 2, 0, GRID_W - NA_COL_K)
    col_ok = (c[:, None] >= col_start[None, :]) & (c[:, None] < col_start[None, :] + NA_COL_K)
    by_offset = jnp.where(col_ok, rpb.astype(F32)[:, :, dc] * LOG2E, NEG)
    masked = jnp.full_like(by_offset[:, 0], NEG)
    tables = []
    for first_row in (0, n, grid_rows - n):
        key_rows = []
        for kr in range(3 * n):
            blocks = []
            for qr in range(n):
                q_abs, k_abs = first_row + qr, first_row - n + kr
                row_start = min(max(q_abs - NA_ROW_K // 2, 0), grid_rows - NA_ROW_K)
                in_window = row_start <= k_abs < row_start + NA_ROW_K
                blocks.append(by_offset[:, k_abs - q_abs + NA_ROW_K - 1] if in_window else masked)
            key_rows.append(jnp.concatenate(blocks, axis=-1))
        tables.append(jnp.concatenate(key_rows, axis=1))
    return jnp.stack(tables)


def _na(q, k, vt, bias):
    b, s, w = q.shape
    blk = NA_ROWS_PER_STEP * GRID_W
    n_blk = s // blk
    clamp = lambda j: jnp.clip(j, 0, n_blk - 1)
    tok = lambda shift: pl.BlockSpec((None, blk, w), lambda bi, j: (bi, clamp(j + shift), 0))
    chan = lambda shift: pl.BlockSpec((None, w, blk), lambda bi, j: (bi, 0, clamp(j + shift)))
    table = pl.BlockSpec(
        (None,) + bias.shape[1:],
        lambda bi, j: (jnp.where(j == 0, 0, jnp.where(j == n_blk - 1, 2, 1)), 0, 0, 0))
    return pl.pallas_call(
        _na_kernel,
        grid=(b, n_blk),
        in_specs=[tok(0), tok(-1), tok(0), tok(1), chan(-1), chan(0), chan(1), table],
        out_specs=tok(0),
        out_shape=jax.ShapeDtypeStruct((b, s, w), BF16),
        compiler_params=pltpu.CompilerParams(
            dimension_semantics=("arbitrary", "arbitrary"), vmem_limit_bytes=40 * MIB),
        name="neighbourhood_attn",
    )(q, k, k, k, vt, vt, vt, bias)


def _sw_kernel(sink_ref, q_ref, kp_ref, kc_ref, kn_ref, vp_ref, vc_ref, vn_ref, o_ref):
    i = pl.program_id(1)
    tq = q_ref.shape[0]
    win = SW_WINDOW
    k = jnp.concatenate([kp_ref[...], kc_ref[...], kn_ref[...]], axis=0)
    vt = jnp.concatenate([vp_ref[...], vc_ref[...], vn_ref[...]], axis=1)
    n_kblk = k.shape[0] // win
    key_row = lax.broadcasted_iota(jnp.int32, (win, win), 0)
    query_col = lax.broadcasted_iota(jnp.int32, (win, win), 1)
    band = {0: jnp.where(key_row >= query_col, 0.0, NEG), 1: None,
            2: jnp.where(key_row <= query_col, 0.0, NEG)}
    outside = {0: jnp.where(i == 0, NEG, 0.0),
               n_kblk - 1: jnp.where(i == pl.num_programs(1) - 1, NEG, 0.0)}
    masks = {}
    for c in range(tq // win):
        for d in range(3):
            a = c + d
            mask = band[d]
            if a in outside:
                mask = outside[a] if mask is None else mask + outside[a]
            masks[a, c] = mask
    low = _low_half_mask()
    group = SW_Q_HEADS // SW_KV_HEADS

    def scores(head):
        kv_head = head // group
        qc = q_ref[:, head // 2 * LANES:(head // 2 + 1) * LANES]
        src = qc if head % 2 == kv_head else pltpu.roll(qc, HEAD_DIM, axis=1)
        qz = jnp.where(low if kv_head == 0 else ~low, src, jnp.zeros_like(src))
        return _dot_nt(k, qz)

    def attend(head, s):
        kv_head = head // group
        sink = sink_ref[0, head]
        p_cols, l_cols = [], []
        for c in range(tq // win):
            blocks = []
            for a in range(c, c + 3):
                blk = s[a * win:(a + 1) * win, c * win:(c + 1) * win]
                blocks.append(blk if masks[a, c] is None else blk + masks[a, c])
            m = jnp.maximum(jnp.max(jnp.maximum(jnp.maximum(blocks[0], blocks[1]), blocks[2]),
                                    axis=0, keepdims=True), sink)
            p = [jnp.exp2(blk - m) for blk in blocks]
            l_cols.append(jnp.sum(p[0] + p[1] + p[2], axis=0, keepdims=True) + jnp.exp2(sink - m))
            empty = jnp.zeros((win, win), BF16)
            p_cols.append(jnp.concatenate(
                [empty] * c + [pb.astype(BF16) for pb in p] + [empty] * (n_kblk - c - 3), axis=0))
        p_all = jnp.concatenate(p_cols, axis=1)
        o = _dot(vt, p_all) / jnp.concatenate(l_cols, axis=1)
        return o[kv_head * HEAD_DIM:(kv_head + 1) * HEAD_DIM]

    res = _head_pipeline(range(SW_Q_HEADS), scores, attend)
    for c in range(B_QW // LANES):
        o_ref[:, c * LANES:(c + 1) * LANES] = (
            jnp.concatenate(res[2 * c:2 * c + 2], axis=0).T.astype(o_ref.dtype))


def _sw(q, k, vt, sink):
    b, s, w = q.shape
    kvw = k.shape[-1]
    tq = SW_Q_TILE
    per = tq // SW_WINDOW
    n_blk = s // SW_WINDOW
    prev = lambda i: jnp.maximum(per * i - 1, 0)
    nxt = lambda i: jnp.minimum(per * (i + 1), n_blk - 1)
    q_spec = pl.BlockSpec((None, tq, w), lambda bi, i: (bi, i, 0))
    k_specs = [pl.BlockSpec((None, SW_WINDOW, kvw), lambda bi, i: (bi, prev(i), 0)),
               pl.BlockSpec((None, tq, kvw), lambda bi, i: (bi, i, 0)),
               pl.BlockSpec((None, SW_WINDOW, kvw), lambda bi, i: (bi, nxt(i), 0))]
    vt_specs = [pl.BlockSpec((None, kvw, SW_WINDOW), lambda bi, i: (bi, 0, prev(i))),
                pl.BlockSpec((None, kvw, tq), lambda bi, i: (bi, 0, i)),
                pl.BlockSpec((None, kvw, SW_WINDOW), lambda bi, i: (bi, 0, nxt(i)))]
    sink_spec = pl.BlockSpec(memory_space=pltpu.SMEM)
    return pl.pallas_call(
        _sw_kernel,
        grid=(b, s // tq),
        in_specs=[sink_spec, q_spec] + k_specs + vt_specs,
        out_specs=q_spec,
        out_shape=jax.ShapeDtypeStruct((b, s, w), BF16),
        compiler_params=pltpu.CompilerParams(dimension_semantics=("arbitrary", "arbitrary")),
        name="sliding_window_attn",
    )(sink, q, k, k, k, vt, vt, vt)


def _diff_kernel(q_ref, k_ref, vt_ref, lq1_ref, lk1_ref, lq2_ref, lk2_ref, g_ref, o_ref,
                 m_ref, *scratch, lam_init):
    tq = DIFF_Q_TILE
    n_sub = q_ref.shape[0] // tq
    per = len(scratch) // n_sub
    subs = [scratch[i * per:(i + 1) * per] for i in range(n_sub)]
    kv_tile = subs[0][2].shape[0]
    n_kv = k_ref.shape[0] // kv_tile
    cols = 2 * tq
    low = _low_half_mask()

    def fold8(x, op):
        return op(x.reshape(x.shape[0] // SUBLANES, SUBLANES, x.shape[1]), axis=0)

    def col_max(x8):
        return jnp.max(x8, axis=0, keepdims=True)

    lam = (jnp.exp(jnp.sum(lq1_ref[...] * lk1_ref[...]))
           - jnp.exp(jnp.sum(lq2_ref[...] * lk2_ref[...])) + lam_init)

    def finalize(sub):
        l_ref, acc_ref = subs[sub][:2]
        o = acc_ref[...] / l_ref[...]
        o = o[:, :tq] - lam * o[:, tq:]
        o = o * lax.rsqrt(jnp.mean(o * o, axis=0, keepdims=True) + EPS) * g_ref[...]
        o_ref[sub * tq:(sub + 1) * tq, :] = (o * (1.0 - lam_init)).T.astype(o_ref.dtype)

    def fast_sweep(sub):
        l_ref, acc_ref, p0_ref, p1_ref, qz_ref = subs[sub]
        p_bufs = (p0_ref, p1_ref)
        q = q_ref[sub * tq:(sub + 1) * tq, :]
        zero = jnp.zeros_like(q)
        qz_ref[0:tq, :] = jnp.where(low, q, zero)
        qz_ref[tq:, :] = jnp.where(low, zero, q)
        ref = None
        unsafe = jnp.zeros((1, cols), F32)
        l8 = None
        beta = None
        pending = None

        def value_product(tile, scale):
            pv = _dot(vt_ref[:, tile * kv_tile:(tile + 1) * kv_tile], p_bufs[tile % 2][...])
            acc_ref[...] = pv if scale is None else scale * acc_ref[...] + pv

        for t in range(n_kv):
            tile_max8 = None
            tile_sum8 = None
            s = _dot_nt(k_ref[t * kv_tile:(t + 1) * kv_tile, :], qz_ref[...])
            if ref is None:
                ref = col_max(fold8(s[0:DIFF_WARM_ROWS], jnp.max))
            if pending is not None:
                value_product(*pending)
            for r in range(0, kv_tile, DIFF_PIECE_ROWS):
                piece = s[r:r + DIFF_PIECE_ROWS]
                mx = fold8(piece, jnp.max)
                tile_max8 = mx if tile_max8 is None else jnp.maximum(tile_max8, mx)
                p = jnp.exp2(piece - ref)
                ps = fold8(p, jnp.sum)
                tile_sum8 = ps if tile_sum8 is None else tile_sum8 + ps
                p_bufs[t % 2][r:r + DIFF_PIECE_ROWS, :] = p.astype(BF16)
            pending = (t, beta)
            l8 = tile_sum8 if beta is None else beta * l8 + tile_sum8
            tile_max = col_max(tile_max8)
            unsafe = jnp.maximum(unsafe, jnp.where(tile_max - ref > DIFF_REF_SLACK, 1.0, 0.0))
            new_ref = jnp.maximum(ref, tile_max)
            beta = jnp.exp2(ref - new_ref)
            ref = new_ref
        value_product(*pending)
        l_fast = jnp.sum(l8, axis=0, keepdims=True)
        l_ref[...] = l_fast
        finite = jnp.isfinite(jnp.sum(acc_ref[...], axis=0, keepdims=True) + l_fast)
        return jnp.maximum(unsafe, jnp.where(finite, 0.0, 1.0))

    def exact_sweep(sub):
        l_ref, acc_ref, _, _, qz_ref = subs[sub]
        m_ref[...] = jnp.full(m_ref.shape, -jnp.inf, F32)
        l_ref[...] = jnp.zeros(l_ref.shape, F32)
        acc_ref[...] = jnp.zeros(acc_ref.shape, F32)

        def tile_step(j, carry):
            rows = pl.ds(pl.multiple_of(j * kv_tile, kv_tile), kv_tile)
            s = _dot_nt(k_ref[rows, :], qz_ref[...])
            m_old = m_ref[...]
            m_new = jnp.maximum(m_old, jnp.max(s, axis=0, keepdims=True))
            alpha = jnp.exp2(m_old - m_new)
            p = jnp.exp2(s - m_new)
            m_ref[...] = m_new
            l_ref[...] = alpha * l_ref[...] + jnp.sum(p, axis=0, keepdims=True)
            acc_ref[...] = alpha * acc_ref[...] + _dot(vt_ref[:, rows], p.astype(BF16))
            return carry

        lax.fori_loop(0, n_kv, tile_step, 0)

    unsafe = None
    for sub in range(n_sub):
        flag = fast_sweep(sub)
        unsafe = flag if unsafe is None else jnp.maximum(unsafe, flag)
        finalize(sub)

    @pl.when(jnp.max(unsafe) > 0.0)
    def _():
        for sub in range(n_sub):
            exact_sweep(sub)
            finalize(sub)


def _diff(q, k, vt, lq1, lk1, lq2, lk2, subln_g, lam_init):
    b, s, w = q.shape
    tq = DIFF_Q_TILE
    dv = 2 * HEAD_DIM
    heads = w // LANES
    kv_tile = min(DIFF_KV_TILE, s)
    n_sub = DIFF_Q_SUBTILES
    q_spec = pl.BlockSpec((None, n_sub * tq, LANES), lambda bi, h, i: (bi, i, h))
    k_spec = pl.BlockSpec((None, s, LANES), lambda bi, h, i: (bi, 0, h))
    vt_spec = pl.BlockSpec((None, dv, s), lambda bi, h, i: (bi, h, 0))
    vec = lambda n: pl.BlockSpec((1, n), lambda bi, h, i: (0, 0))
    sub_scratch = [pltpu.VMEM((1, 2 * tq), F32), pltpu.VMEM((dv, 2 * tq), F32),
                   pltpu.VMEM((kv_tile, 2 * tq), BF16), pltpu.VMEM((kv_tile, 2 * tq), BF16),
                   pltpu.VMEM((2 * tq, LANES), BF16)]
    return pl.pallas_call(
        functools.partial(_diff_kernel, lam_init=lam_init),
        grid=(b, heads, s // (n_sub * tq)),
        in_specs=[q_spec, k_spec, vt_spec, vec(HEAD_DIM), vec(HEAD_DIM), vec(HEAD_DIM),
                  vec(HEAD_DIM), pl.BlockSpec((dv, 1), lambda bi, h, i: (0, 0))],
        out_specs=q_spec,
        out_shape=jax.ShapeDtypeStruct((b, s, w), BF16),
        scratch_shapes=[pltpu.VMEM((1, 2 * tq), F32)] + sub_scratch * n_sub,
        compiler_params=pltpu.CompilerParams(
            dimension_semantics=("arbitrary", "arbitrary", "arbitrary"),
            vmem_limit_bytes=32 * MIB),
        name="differential_attn",
    )(q, k, vt, lq1, lk1, lq2, lk2, subln_g)


def _merge_kernel(h_ref, u_ref, oa_ref, ob_ref, oc_ref, wg_ref, bg_ref, wb_ref, wo_ref, g_ref,
                  o_ref):
    d = h_ref.shape[1]
    u = u_ref[...]
    merged = None
    for n, br_ref in enumerate((oa_ref, ob_ref, oc_ref)):
        gate = jax.nn.sigmoid(_dot(u, wg_ref[:, n * d:(n + 1) * d]) + bg_ref[n:n + 1, :])
        term = gate * _dot(br_ref[...], wb_ref[n])
        merged = term if merged is None else merged + term
    y = _dot(merged.astype(BF16), wo_ref[...])
    o_ref[...] = h_ref[...] + _rmsnorm(y, g_ref[...])


def _merge(h, u, oa, ob, oc, w_gate, b_gate, w_branch, w_out, post_g):
    t, d = h.shape
    rows = lambda w: pl.BlockSpec((TOKEN_TILE, w), lambda i: (i, 0))
    return pl.pallas_call(
        _merge_kernel,
        grid=(t // TOKEN_TILE,),
        in_specs=[rows(d), rows(d), rows(oa.shape[1]), rows(ob.shape[1]), rows(oc.shape[1]),
                  _resident(w_gate.shape), _resident(b_gate.shape), _resident(w_branch.shape),
                  _resident(w_out.shape), _resident((1, d))],
        out_specs=rows(d),
        out_shape=jax.ShapeDtypeStruct((t, d), F32),
        compiler_params=pltpu.CompilerParams(
            dimension_semantics=("arbitrary",), vmem_limit_bytes=40 * MIB),
        name="gated_merge",
    )(h, u, oa, ob, oc, w_gate, b_gate, w_branch, w_out, post_g)


def _rope_tables(s):
    pos = jnp.arange(s, dtype=F32)
    inv = ROPE_THETA ** (-jnp.arange(0, HEAD_DIM, 2, dtype=F32) / HEAD_DIM)
    ang = pos[:, None] * inv[None, :]
    cos = jnp.tile(jnp.cos(ang), (1, 2 * LANES // HEAD_DIM))
    sin = jnp.tile(jnp.sin(ang), (1, 2 * LANES // HEAD_DIM))
    first_half = (jnp.arange(LANES) % HEAD_DIM) < HEAD_DIM // 2
    return cos, jnp.where(first_half, -sin, 0.0), jnp.where(first_half, 0.0, sin)


def kernel(x, ffn1_pre_g, ffn1_w_gu, ffn1_w_down, ffn1_post_g, mix_pre_g, w_in, na_rpb, sw_sink, diff_lambda_q1, diff_lambda_k1, diff_lambda_q2, diff_lambda_k2, diff_subln_g, w_branch, b_gate, w_out, mix_post_g, ffn2_pre_g, ffn2_w_gu, ffn2_w_down, ffn2_post_g):
    b, s, d = x.shape
    depth = w_in.shape[0]
    assert s % (NA_ROWS_PER_STEP * GRID_W) == 0 and s % TOKEN_TILE == 0 and s >= NA_ROW_K * GRID_W
    cos, sin_lo, sin_hi = _rope_tables(s)
    row = lambda a: a.reshape(1, -1).astype(F32)
    h = x.reshape(b * s, d)
    for l in range(depth):
        h = _ffn(h, row(ffn1_pre_g[l]), ffn1_w_gu[l].astype(BF16), ffn1_w_down[l].astype(BF16),
                 row(ffn1_post_g[l]))

        w_l = w_in[l]
        qa_w, ka_w, va_w, qb_w, kb_w, vb_w, qc_w, kc_w, vc_w, gate_w = (
            w_l[:, c0:c1] for c0, c1 in zip(_SPLIT_EDGES[:-1], _SPLIT_EDGES[1:]))
        w_qk = jnp.concatenate([qa_w, ka_w, qb_w, kb_w, qc_w, kc_w], axis=1).astype(BF16)
        w_vt = jnp.concatenate([va_w, vb_w, vc_w], axis=1).T.astype(BF16)
        u, qa, ka, qb, kb, qc, kc, vat, vbt, vct = _proj(
            h, row(mix_pre_g[l]), w_qk, w_vt, cos, sin_lo, sin_hi)
        seq = lambda a: a.reshape(b, s, a.shape[-1])
        oa = _na(seq(qa), seq(ka), vat, _na_bias_table(na_rpb[l], s // GRID_W))
        ob = _sw(seq(qb), seq(kb), vbt, sw_sink[l].reshape(1, -1).astype(F32) * LOG2E)
        lam_init = 0.8 - 0.6 * math.exp(-0.3 * l)
        oc = _diff(seq(qc), seq(kc), vct, row(diff_lambda_q1[l]), row(diff_lambda_k1[l]),
                   row(diff_lambda_q2[l]), row(diff_lambda_k2[l]),
                   diff_subln_g[l].reshape(-1, 1).astype(F32), lam_init)
        flat = lambda a: a.reshape(b * s, a.shape[-1])
        h = _merge(h, u, flat(oa), flat(ob), flat(oc), gate_w.astype(BF16), b_gate[l].astype(F32),
                   w_branch[l].astype(BF16), w_out[l].astype(BF16), row(mix_post_g[l]))

        h = _ffn(h, row(ffn2_pre_g[l]), ffn2_w_gu[l].astype(BF16), ffn2_w_down[l].astype(BF16),
                 row(ffn2_post_g[l]))
    return h.reshape(b, s, d)
```

```python
import functools
import math

import jax
import jax.numpy as jnp
import numpy as np
from jax import lax
from jax.experimental import pallas as pl
from jax.experimental.pallas import tpu as pltpu

F32 = jnp.float32
BF16 = jnp.bfloat16

HEAD_DIM = 64
GRID_W = 64
NA_HEADS = 8
NA_ROW_K = 8
NA_COL_K = 16
SW_Q_HEADS = 8
SW_KV_HEADS = 2
SW_WINDOW = 128
DIFF_HEADS = 4
N_BRANCH = 3
ROPE_THETA = 10000.0
EPS = 1e-6
LOG2E = math.log2(math.e)
NEG = -1e30
LANES = 128
MIB = 1024 * 1024

A_W = NA_HEADS * HEAD_DIM
B_QW = SW_Q_HEADS * HEAD_DIM
B_KVW = SW_KV_HEADS * HEAD_DIM
C_W = 2 * DIFF_HEADS * HEAD_DIM
_SPLIT_EDGES = (0, A_W, 2 * A_W, 3 * A_W, 3 * A_W + B_QW, 3 * A_W + B_QW + B_KVW,
                3 * A_W + B_QW + 2 * B_KVW, 3 * A_W + B_QW + 2 * B_KVW + C_W,
                3 * A_W + B_QW + 2 * B_KVW + 2 * C_W, 3 * A_W + B_QW + 2 * B_KVW + 3 * C_W, None)

TOKEN_TILE = 512
FFN_TILE = 1024
FFN_SUB_ROWS = 256
NA_ROWS_PER_STEP = 4
SW_Q_TILE = 256
DIFF_Q_TILE = 256
DIFF_KV_TILE = 512
DIFF_Q_SUBTILES = 2
DIFF_PIECE_ROWS = 16
DIFF_WARM_ROWS = 128
DIFF_REF_SLACK = 40.0
SUBLANES = 8
HEAD_PIPELINE_DEPTH = 4


def _rmsnorm(x, g):
    return x * lax.rsqrt(jnp.mean(x * x, axis=-1, keepdims=True) + EPS) * g


def _dot(a, b):
    return jnp.dot(a, b, preferred_element_type=F32)


def _dot_nt(a, b):
    return lax.dot_general(a, b, (((1,), (1,)), ((), ())), preferred_element_type=F32)


def _resident(shape):
    return pl.BlockSpec(shape, lambda *_: (0,) * len(shape), pipeline_mode=pl.Buffered(1))


def _low_half_mask():
    return lax.broadcasted_iota(jnp.int32, (1, LANES), 1) < HEAD_DIM


def _head_pipeline(heads, scores, attend, depth=HEAD_PIPELINE_DEPTH):
    heads = list(heads)
    ready = [scores(h) for h in heads[:depth]]
    out = []
    for idx, head in enumerate(heads):
        if idx + depth < len(heads):
            ready.append(scores(heads[idx + depth]))
        out.append(attend(head, ready.pop(0)))
    return out


def _ffn_kernel(x_ref, pre_g_ref, wgu_ref, wdown_ref, post_g_ref, o_ref, *, f_chunks, d_ff):
    for r0 in range(0, x_ref.shape[0], FFN_SUB_ROWS):
        rows = slice(r0, r0 + FFN_SUB_ROWS)
        x = x_ref[rows, :]
        xn = _rmsnorm(x, pre_g_ref[...]).astype(BF16)
        acc = None
        for c0, c1 in f_chunks:
            gate = _dot(xn, wgu_ref[:, c0:c1])
            up = _dot(xn, wgu_ref[:, d_ff + c0:d_ff + c1])
            act = (gate * jax.nn.sigmoid(gate) * up).astype(BF16)
            part = _dot(act, wdown_ref[c0:c1, :])
            acc = part if acc is None else acc + part
        o_ref[rows, :] = x + 0.5 * _rmsnorm(acc, post_g_ref[...])


def _ffn(h, pre_g, w_gu, w_down, post_g):
    t, d = h.shape
    d_ff = w_down.shape[0]
    step = 1024
    f_chunks = tuple((c, min(c + step, d_ff)) for c in range(0, d_ff, step))
    row = pl.BlockSpec((FFN_TILE, d), lambda i: (i, 0))
    return pl.pallas_call(
        functools.partial(_ffn_kernel, f_chunks=f_chunks, d_ff=d_ff),
        grid=(t // FFN_TILE,),
        in_specs=[row, _resident((1, d)), _resident(w_gu.shape), _resident(w_down.shape),
                  _resident((1, d))],
        out_specs=row,
        out_shape=jax.ShapeDtypeStruct((t, d), F32),
        compiler_params=pltpu.CompilerParams(
            dimension_semantics=("arbitrary",), vmem_limit_bytes=48 * MIB),
        name="ffn",
    )(h, pre_g, w_gu, w_down, post_g)


def _rope(t, cos, sin_lo, sin_hi):
    out = []
    for c in range(t.shape[1] // LANES):
        tc = t[:, c * LANES:(c + 1) * LANES]
        fwd = pltpu.roll(tc, LANES - HEAD_DIM // 2, axis=1)
        back = pltpu.roll(tc, HEAD_DIM // 2, axis=1)
        out.append(tc * cos + fwd * sin_lo + back * sin_hi)
    return jnp.concatenate(out, axis=1)


def _proj_kernel(h_ref, g_ref, wqk_ref, wvt_ref, cos_ref, sin_lo_ref, sin_hi_ref,
                 u_ref, qa_ref, ka_ref, qb_ref, kb_ref, qc_ref, kc_ref, vat_ref, vbt_ref, vct_ref):
    scale = HEAD_DIM ** -0.5 * LOG2E
    for r0 in range(0, h_ref.shape[0], FFN_SUB_ROWS):
        rows = slice(r0, r0 + FFN_SUB_ROWS)
        un = _rmsnorm(h_ref[rows, :], g_ref[...]).astype(BF16)
        u_ref[rows, :] = un
        cos, sin_lo, sin_hi = cos_ref[rows, :], sin_lo_ref[rows, :], sin_hi_ref[rows, :]
        col = 0
        for out_ref, rotary, scaled in ((qa_ref, False, True), (ka_ref, False, False),
                                        (qb_ref, True, True), (kb_ref, True, False),
                                        (qc_ref, True, True), (kc_ref, True, False)):
            width = out_ref.shape[1]
            t = _dot(un, wqk_ref[:, col:col + width])
            col += width
            if rotary:
                t = _rope(t, cos, sin_lo, sin_hi)
            out_ref[rows, :] = (t * scale if scaled else t).astype(BF16)
        row = 0
        for vt_ref in (vat_ref, vbt_ref, vct_ref):
            width = vt_ref.shape[0]
            vt_ref[:, rows] = _dot_nt(wvt_ref[row:row + width, :], un).astype(BF16)
            row += width


def _proj(h, g, w_qk, w_vt, cos, sin_lo, sin_hi):
    t, d = h.shape
    s = cos.shape[0]
    tiles_per_seq = s // TOKEN_TILE
    widths = (d, A_W, A_W, B_QW, B_KVW, C_W, C_W)
    v_widths = (A_W, B_KVW, C_W)
    rows = lambda w: pl.BlockSpec((TOKEN_TILE, w), lambda i: (i, 0))
    chans = lambda w: pl.BlockSpec((None, w, TOKEN_TILE),
                                   lambda i: (i // tiles_per_seq, 0, i % tiles_per_seq))
    table = pl.BlockSpec((TOKEN_TILE, LANES), lambda i: (i % tiles_per_seq, 0))
    return pl.pallas_call(
        _proj_kernel,
        grid=(t // TOKEN_TILE,),
        in_specs=[rows(d), _resident((1, d)), _resident(w_qk.shape), _resident(w_vt.shape),
                  table, table, table],
        out_specs=[rows(w) for w in widths] + [chans(w) for w in v_widths],
        out_shape=([jax.ShapeDtypeStruct((t, w), BF16) for w in widths]
                   + [jax.ShapeDtypeStruct((t // s, w, s), BF16) for w in v_widths]),
        compiler_params=pltpu.CompilerParams(
            dimension_semantics=("arbitrary",), vmem_limit_bytes=40 * MIB),
        name="proj",
    )(h, g, w_qk, w_vt, cos, sin_lo, sin_hi)


def _na_kernel(q_ref, kp_ref, kc_ref, kn_ref, vtp_ref, vtc_ref, vtn_ref, bias_ref, o_ref):
    blk = q_ref.shape[0]
    k_refs = (kp_ref, kc_ref, kn_ref)
    vt_refs = (vtp_ref, vtc_ref, vtn_ref)
    low = _low_half_mask()
    top = lax.broadcasted_iota(jnp.int32, (LANES, 1), 0) < HEAD_DIM
    lanes = lambda head: slice(head // 2 * LANES, (head // 2 + 1) * LANES)

    def scores(head):
        qc = q_ref[:, lanes(head)]
        qz = jnp.where(low if head % 2 == 0 else ~low, qc, jnp.zeros_like(qc))
        return [_dot_nt(k_refs[w][:, lanes(head)], qz) + bias_ref[head, w * blk:(w + 1) * blk, :]
                for w in range(3)]

    def attend(head, s):
        m = jnp.max(jnp.maximum(jnp.maximum(s[0], s[1]), s[2]), axis=0, keepdims=True)
        p = [jnp.exp2(sw - m) for sw in s]
        l = jnp.sum(p[0] + p[1] + p[2], axis=0, keepdims=True)
        o = sum(_dot(vt_refs[w][lanes(head), :], p[w].astype(BF16)) for w in range(3))
        return o / l

    res = _head_pipeline(range(NA_HEADS), scores, attend)
    for c in range(A_W // LANES):
        o_ref[:, c * LANES:(c + 1) * LANES] = (
            jnp.where(top, res[2 * c], res[2 * c + 1]).T.astype(o_ref.dtype))


def _na_bias_table(rpb, grid_rows):
    n = NA_ROWS_PER_STEP
    c = np.arange(GRID_W)
    dc = np.clip(c[:, None] - c[None, :] + (NA_COL_K - 1), 0, 2 * NA_COL_K - 2)
    col_start = np.clip(c - NA_COL_K // 2, 0, GRID_W - NA_COL_K)
    col_ok = (c[:, None] >= col_start[None, :]) & (c[:, None] < col_start[None, :] + NA_COL_K)
    one_hot = jnp.asarray(dc[None] == np.arange(2 * NA_COL_K - 1)[:, None, None], F32)
    expanded = jnp.einsum("hdj,jkq->hdkq", rpb.astype(F32), one_hot,
                          precision=lax.Precision.HIGHEST)
    by_offset = jnp.where(col_ok, expanded * LOG2E, NEG)
    masked = jnp.full_like(by_offset[:, 0], NEG)
    tables = []
    for first_row in (0, n, grid_rows - n):
        key_rows = []
        for kr in range(3 * n):
            blocks = []
            for qr in range(n):
                q_abs, k_abs = first_row + qr, first_row - n + kr
                row_start = min(max(q_abs - NA_ROW_K // 2, 0), grid_rows - NA_ROW_K)
                in_window = row_start <= k_abs < row_start + NA_ROW_K
                blocks.append(by_offset[:, k_abs - q_abs + NA_ROW_K - 1] if in_window else masked)
            key_rows.append(jnp.concatenate(blocks, axis=-1))
        tables.append(jnp.concatenate(key_rows, axis=1))
    return jnp.stack(tables)


def _na(q, k, vt, bias):
    b, s, w = q.shape
    blk = NA_ROWS_PER_STEP * GRID_W
    n_blk = s // blk
    clamp = lambda j: jnp.clip(j, 0, n_blk - 1)
    tok = lambda shift: pl.BlockSpec((None, blk, w), lambda bi, j: (bi, clamp(j + shift), 0))
    chan = lambda shift: pl.BlockSpec((None, w, blk), lambda bi, j: (bi, 0, clamp(j + shift)))
    table = pl.BlockSpec(
        (None,) + bias.shape[1:],
        lambda bi, j: (jnp.where(j == 0, 0, jnp.where(j == n_blk - 1, 2, 1)), 0, 0, 0))
    return pl.pallas_call(
        _na_kernel,
        grid=(b, n_blk),
        in_specs=[tok(0), tok(-1), tok(0), tok(1), chan(-1), chan(0), chan(1), table],
        out_specs=tok(0),
        out_shape=jax.ShapeDtypeStruct((b, s, w), BF16),
        compiler_params=pltpu.CompilerParams(
            dimension_semantics=("arbitrary", "arbitrary"), vmem_limit_bytes=40 * MIB),
        name="neighbourhood_attn",
    )(q, k, k, k, vt, vt, vt, bias)


def _sw_kernel(sink_ref, q_ref, kp_ref, kc_ref, kn_ref, vp_ref, vc_ref, vn_ref, o_ref):
    i = pl.program_id(1)
    tq = q_ref.shape[0]
    win = SW_WINDOW
    k = jnp.concatenate([kp_ref[...], kc_ref[...], kn_ref[...]], axis=0)
    vt = jnp.concatenate([vp_ref[...], vc_ref[...], vn_ref[...]], axis=1)
    n_kblk = k.shape[0] // win
    key_row = lax.broadcasted_iota(jnp.int32, (win, win), 0)
    query_col = lax.broadcasted_iota(jnp.int32, (win, win), 1)
    band = {0: jnp.where(key_row >= query_col, 0.0, NEG), 1: None,
            2: jnp.where(key_row <= query_col, 0.0, NEG)}
    outside = {0: jnp.where(i == 0, NEG, 0.0),
               n_kblk - 1: jnp.where(i == pl.num_programs(1) - 1, NEG, 0.0)}
    masks = {}
    for c in range(tq // win):
        for d in range(3):
            a = c + d
            mask = band[d]
            if a in outside:
                mask = outside[a] if mask is None else mask + outside[a]
            masks[a, c] = mask
    low = _low_half_mask()
    group = SW_Q_HEADS // SW_KV_HEADS

    def scores(head):
        kv_head = head // group
        qc = q_ref[:, head // 2 * LANES:(head // 2 + 1) * LANES]
        src = qc if head % 2 == kv_head else pltpu.roll(qc, HEAD_DIM, axis=1)
        qz = jnp.where(low if kv_head == 0 else ~low, src, jnp.zeros_like(src))
        return _dot_nt(k, qz)

    def attend(head, s):
        kv_head = head // group
        sink = sink_ref[0, head]
        p_cols, l_cols = [], []
        for c in range(tq // win):
            blocks = []
            for a in range(c, c + 3):
                blk = s[a * win:(a + 1) * win, c * win:(c + 1) * win]
                blocks.append(blk if masks[a, c] is None else blk + masks[a, c])
            m = jnp.maximum(jnp.max(jnp.maximum(jnp.maximum(blocks[0], blocks[1]), blocks[2]),
                                    axis=0, keepdims=True), sink)
            p = [jnp.exp2(blk - m) for blk in blocks]
            l_cols.append(jnp.sum(p[0] + p[1] + p[2], axis=0, keepdims=True) + jnp.exp2(sink - m))
            empty = jnp.zeros((win, win), BF16)
            p_cols.append(jnp.concatenate(
                [empty] * c + [pb.astype(BF16) for pb in p] + [empty] * (n_kblk - c - 3), axis=0))
        p_all = jnp.concatenate(p_cols, axis=1)
        o = _dot(vt, p_all) / jnp.concatenate(l_cols, axis=1)
        return o[kv_head * HEAD_DIM:(kv_head + 1) * HEAD_DIM]

    res = _head_pipeline(range(SW_Q_HEADS), scores, attend)
    for c in range(B_QW // LANES):
        o_ref[:, c * LANES:(c + 1) * LANES] = (
            jnp.concatenate(res[2 * c:2 * c + 2], axis=0).T.astype(o_ref.dtype))


def _sw(q, k, vt, sink):
    b, s, w = q.shape
    kvw = k.shape[-1]
    tq = SW_Q_TILE
    per = tq // SW_WINDOW
    n_blk = s // SW_WINDOW
    prev = lambda i: jnp.maximum(per * i - 1, 0)
    nxt = lambda i: jnp.minimum(per * (i + 1), n_blk - 1)
    q_spec = pl.BlockSpec((None, tq, w), lambda bi, i: (bi, i, 0))
    k_specs = [pl.BlockSpec((None, SW_WINDOW, kvw), lambda bi, i: (bi, prev(i), 0)),
               pl.BlockSpec((None, tq, kvw), lambda bi, i: (bi, i, 0)),
               pl.BlockSpec((None, SW_WINDOW, kvw), lambda bi, i: (bi, nxt(i), 0))]
    vt_specs = [pl.BlockSpec((None, kvw, SW_WINDOW), lambda bi, i: (bi, 0, prev(i))),
                pl.BlockSpec((None, kvw, tq), lambda bi, i: (bi, 0, i)),
                pl.BlockSpec((None, kvw, SW_WINDOW), lambda bi, i: (bi, 0, nxt(i)))]
    sink_spec = pl.BlockSpec(memory_space=pltpu.SMEM)
    return pl.pallas_call(
        _sw_kernel,
        grid=(b, s // tq),
        in_specs=[sink_spec, q_spec] + k_specs + vt_specs,
        out_specs=q_spec,
        out_shape=jax.ShapeDtypeStruct((b, s, w), BF16),
        compiler_params=pltpu.CompilerParams(dimension_semantics=("arbitrary", "arbitrary")),
        name="sliding_window_attn",
    )(sink, q, k, k, k, vt, vt, vt)


def _diff_kernel(q_ref, k_ref, vt_ref, lq1_ref, lk1_ref, lq2_ref, lk2_ref, g_ref, o_ref,
                 m_ref, *scratch, lam_init):
    tq = DIFF_Q_TILE
    n_sub = q_ref.shape[0] // tq
    per = len(scratch) // n_sub
    subs = [scratch[i * per:(i + 1) * per] for i in range(n_sub)]
    kv_tile = subs[0][2].shape[0]
    n_kv = k_ref.shape[0] // kv_tile
    cols = 2 * tq
    low = _low_half_mask()

    def fold8(x, op):
        return op(x.reshape(x.shape[0] // SUBLANES, SUBLANES, x.shape[1]), axis=0)

    def col_max(x8):
        return jnp.max(x8, axis=0, keepdims=True)

    lam = (jnp.exp(jnp.sum(lq1_ref[...] * lk1_ref[...]))
           - jnp.exp(jnp.sum(lq2_ref[...] * lk2_ref[...])) + lam_init)

    def finalize(sub):
        l_ref, acc_ref = subs[sub][:2]
        o = acc_ref[...] / l_ref[...]
        o = o[:, :tq] - lam * o[:, tq:]
        o = o * lax.rsqrt(jnp.mean(o * o, axis=0, keepdims=True) + EPS) * g_ref[...]
        o_ref[sub * tq:(sub + 1) * tq, :] = (o * (1.0 - lam_init)).T.astype(o_ref.dtype)

    def fast_sweep(sub):
        l_ref, acc_ref, p0_ref, p1_ref, qz_ref = subs[sub]
        p_bufs = (p0_ref, p1_ref)
        q = q_ref[sub * tq:(sub + 1) * tq, :]
        zero = jnp.zeros_like(q)
        qz_ref[0:tq, :] = jnp.where(low, q, zero)
        qz_ref[tq:, :] = jnp.where(low, zero, q)
        ref = None
        unsafe = jnp.zeros((1, cols), F32)
        l8 = None
        beta = None
        pending = None

        def value_product(tile, scale):
            pv = _dot(vt_ref[:, tile * kv_tile:(tile + 1) * kv_tile], p_bufs[tile % 2][...])
            acc_ref[...] = pv if scale is None else scale * acc_ref[...] + pv

        for t in range(n_kv):
            tile_max8 = None
            tile_sum8 = None
            s = _dot_nt(k_ref[t * kv_tile:(t + 1) * kv_tile, :], qz_ref[...])
            if ref is None:
                ref = col_max(fold8(s[0:DIFF_WARM_ROWS], jnp.max))
            if pending is not None:
                value_product(*pending)
            for r in range(0, kv_tile, DIFF_PIECE_ROWS):
                piece = s[r:r + DIFF_PIECE_ROWS]
                mx = fold8(piece, jnp.max)
                tile_max8 = mx if tile_max8 is None else jnp.maximum(tile_max8, mx)
                p = jnp.exp2(piece - ref)
                ps = fold8(p, jnp.sum)
                tile_sum8 = ps if tile_sum8 is None else tile_sum8 + ps
                p_bufs[t % 2][r:r + DIFF_PIECE_ROWS, :] = p.astype(BF16)
            pending = (t, beta)
            l8 = tile_sum8 if beta is None else beta * l8 + tile_sum8
            tile_max = col_max(tile_max8)
            unsafe = jnp.maximum(unsafe, jnp.where(tile_max - ref > DIFF_REF_SLACK, 1.0, 0.0))
            new_ref = jnp.maximum(ref, tile_max)
            beta = jnp.exp2(ref - new_ref)
            ref = new_ref
        value_product(*pending)
        l_fast = jnp.sum(l8, axis=0, keepdims=True)
        l_ref[...] = l_fast
        finite = jnp.isfinite(jnp.sum(acc_ref[...], axis=0, keepdims=True) + l_fast)
        return jnp.maximum(unsafe, jnp.where(finite, 0.0, 1.0))

    def exact_sweep(sub):
        l_ref, acc_ref, _, _, qz_ref = subs[sub]
        m_ref[...] = jnp.full(m_ref.shape, -jnp.inf, F32)
        l_ref[...] = jnp.zeros(l_ref.shape, F32)
        acc_ref[...] = jnp.zeros(acc_ref.shape, F32)

        def tile_step(j, carry):
            rows = pl.ds(pl.multiple_of(j * kv_tile, kv_tile), kv_tile)
            s = _dot_nt(k_ref[rows, :], qz_ref[...])
            m_old = m_ref[...]
            m_new = jnp.maximum(m_old, jnp.max(s, axis=0, keepdims=True))
            alpha = jnp.exp2(m_old - m_new)
            p = jnp.exp2(s - m_new)
            m_ref[...] = m_new
            l_ref[...] = alpha * l_ref[...] + jnp.sum(p, axis=0, keepdims=True)
            acc_ref[...] = alpha * acc_ref[...] + _dot(vt_ref[:, rows], p.astype(BF16))
            return carry

        lax.fori_loop(0, n_kv, tile_step, 0)

    unsafe = None
    for sub in range(n_sub):
        flag = fast_sweep(sub)
        unsafe = flag if unsafe is None else jnp.maximum(unsafe, flag)
        finalize(sub)

    @pl.when(jnp.max(unsafe) > 0.0)
    def _():
        for sub in range(n_sub):
            exact_sweep(sub)
            finalize(sub)


def _diff(q, k, vt, lq1, lk1, lq2, lk2, subln_g, lam_init):
    b, s, w = q.shape
    tq = DIFF_Q_TILE
    dv = 2 * HEAD_DIM
    heads = w // LANES
    kv_tile = min(DIFF_KV_TILE, s)
    n_sub = DIFF_Q_SUBTILES
    q_spec = pl.BlockSpec((None, n_sub * tq, LANES), lambda bi, h, i: (bi, i, h))
    k_spec = pl.BlockSpec((None, s, LANES), lambda bi, h, i: (bi, 0, h))
    vt_spec = pl.BlockSpec((None, dv, s), lambda bi, h, i: (bi, h, 0))
    vec = lambda n: pl.BlockSpec((1, n), lambda bi, h, i: (0, 0))
    sub_scratch = [pltpu.VMEM((1, 2 * tq), F32), pltpu.VMEM((dv, 2 * tq), F32),
                   pltpu.VMEM((kv_tile, 2 * tq), BF16), pltpu.VMEM((kv_tile, 2 * tq), BF16),
                   pltpu.VMEM((2 * tq, LANES), BF16)]
    return pl.pallas_call(
        functools.partial(_diff_kernel, lam_init=lam_init),
        grid=(b, heads, s // (n_sub * tq)),
        in_specs=[q_spec, k_spec, vt_spec, vec(HEAD_DIM), vec(HEAD_DIM), vec(HEAD_DIM),
                  vec(HEAD_DIM), pl.BlockSpec((dv, 1), lambda bi, h, i: (0, 0))],
        out_specs=q_spec,
        out_shape=jax.ShapeDtypeStruct((b, s, w), BF16),
        scratch_shapes=[pltpu.VMEM((1, 2 * tq), F32)] + sub_scratch * n_sub,
        compiler_params=pltpu.CompilerParams(
            dimension_semantics=("arbitrary", "arbitrary", "arbitrary"),
            vmem_limit_bytes=32 * MIB),
        name="differential_attn",
    )(q, k, vt, lq1, lk1, lq2, lk2, subln_g)


def _merge_kernel(h_ref, u_ref, oa_ref, ob_ref, oc_ref, wg_ref, bg_ref, wb_ref, wo_ref, g_ref,
                  o_ref):
    d = h_ref.shape[1]
    for r0 in range(0, h_ref.shape[0], FFN_SUB_ROWS):
        rows = slice(r0, r0 + FFN_SUB_ROWS)
        u = u_ref[rows, :]
        merged = None
        for n, br_ref in enumerate((oa_ref, ob_ref, oc_ref)):
            gate = jax.nn.sigmoid(_dot(u, wg_ref[:, n * d:(n + 1) * d]) + bg_ref[n:n + 1, :])
            term = gate * _dot(br_ref[rows, :], wb_ref[n])
            merged = term if merged is None else merged + term
        y = _dot(merged.astype(BF16), wo_ref[...])
        o_ref[rows, :] = h_ref[rows, :] + _rmsnorm(y, g_ref[...])


def _merge(h, u, oa, ob, oc, w_gate, b_gate, w_branch, w_out, post_g):
    t, d = h.shape
    rows = lambda w: pl.BlockSpec((TOKEN_TILE, w), lambda i: (i, 0))
    return pl.pallas_call(
        _merge_kernel,
        grid=(t // TOKEN_TILE,),
        in_specs=[rows(d), rows(d), rows(oa.shape[1]), rows(ob.shape[1]), rows(oc.shape[1]),
                  _resident(w_gate.shape), _resident(b_gate.shape), _resident(w_branch.shape),
                  _resident(w_out.shape), _resident((1, d))],
        out_specs=rows(d),
        out_shape=jax.ShapeDtypeStruct((t, d), F32),
        compiler_params=pltpu.CompilerParams(
            dimension_semantics=("arbitrary",), vmem_limit_bytes=40 * MIB),
        name="gated_merge",
    )(h, u, oa, ob, oc, w_gate, b_gate, w_branch, w_out, post_g)


def _rope_tables(s):
    pos = jnp.arange(s, dtype=F32)
    inv = ROPE_THETA ** (-jnp.arange(0, HEAD_DIM, 2, dtype=F32) / HEAD_DIM)
    ang = pos[:, None] * inv[None, :]
    cos = jnp.tile(jnp.cos(ang), (1, 2 * LANES // HEAD_DIM))
    sin = jnp.tile(jnp.sin(ang), (1, 2 * LANES // HEAD_DIM))
    first_half = (jnp.arange(LANES) % HEAD_DIM) < HEAD_DIM // 2
    return cos, jnp.where(first_half, -sin, 0.0), jnp.where(first_half, 0.0, sin)


def kernel(x, ffn1_pre_g, ffn1_w_gu, ffn1_w_down, ffn1_post_g, mix_pre_g, w_in, na_rpb, sw_sink, diff_lambda_q1, diff_lambda_k1, diff_lambda_q2, diff_lambda_k2, diff_subln_g, w_branch, b_gate, w_out, mix_post_g, ffn2_pre_g, ffn2_w_gu, ffn2_w_down, ffn2_post_g):
    b, s, d = x.shape
    depth = w_in.shape[0]
    assert s % (NA_ROWS_PER_STEP * GRID_W) == 0 and s % TOKEN_TILE == 0 and s >= NA_ROW_K * GRID_W
    cos, sin_lo, sin_hi = _rope_tables(s)
    row = lambda a: a.reshape(1, -1).astype(F32)
    h = x.reshape(b * s, d)
    for l in range(depth):
        h = _ffn(h, row(ffn1_pre_g[l]), ffn1_w_gu[l].astype(BF16), ffn1_w_down[l].astype(BF16),
                 row(ffn1_post_g[l]))

        w_l = w_in[l]
        qa_w, ka_w, va_w, qb_w, kb_w, vb_w, qc_w, kc_w, vc_w, gate_w = (
            w_l[:, c0:c1] for c0, c1 in zip(_SPLIT_EDGES[:-1], _SPLIT_EDGES[1:]))
        w_qk = jnp.concatenate([qa_w, ka_w, qb_w, kb_w, qc_w, kc_w], axis=1).astype(BF16)
        w_vt = jnp.concatenate([va_w, vb_w, vc_w], axis=1).T.astype(BF16)
        u, qa, ka, qb, kb, qc, kc, vat, vbt, vct = _proj(
            h, row(mix_pre_g[l]), w_qk, w_vt, cos, sin_lo, sin_hi)
        seq = lambda a: a.reshape(b, s, a.shape[-1])
        oa = _na(seq(qa), seq(ka), vat, _na_bias_table(na_rpb[l], s // GRID_W))
        ob = _sw(seq(qb), seq(kb), vbt, sw_sink[l].reshape(1, -1).astype(F32) * LOG2E)
        lam_init = 0.8 - 0.6 * math.exp(-0.3 * l)
        oc = _diff(seq(qc), seq(kc), vct, row(diff_lambda_q1[l]), row(diff_lambda_k1[l]),
                   row(diff_lambda_q2[l]), row(diff_lambda_k2[l]),
                   diff_subln_g[l].reshape(-1, 1).astype(F32), lam_init)
        flat = lambda a: a.reshape(b * s, a.shape[-1])
        h = _merge(h, u, flat(oa), flat(ob), flat(oc), gate_w.astype(BF16), b_gate[l].astype(F32),
                   w_branch[l].astype(BF16), w_out[l].astype(BF16), row(mix_post_g[l]))

        h = _ffn(h, row(ffn2_pre_g[l]), ffn2_w_gu[l].astype(BF16), ffn2_w_down[l].astype(BF16),
                 row(ffn2_post_g[l]))
    return h.reshape(b, s, d)
```

```python
import functools
import math

import jax
import jax.numpy as jnp
import numpy as np
from jax import lax
from jax.experimental import pallas as pl
from jax.experimental.pallas import tpu as pltpu

F32 = jnp.float32
BF16 = jnp.bfloat16

HEAD_DIM = 64
GRID_W = 64
NA_HEADS = 8
NA_ROW_K = 8
NA_COL_K = 16
SW_Q_HEADS = 8
SW_KV_HEADS = 2
SW_WINDOW = 128
DIFF_HEADS = 4
N_BRANCH = 3
ROPE_THETA = 10000.0
EPS = 1e-6
LOG2E = math.log2(math.e)
NEG = -1e30
LANES = 128
MIB = 1024 * 1024

A_W = NA_HEADS * HEAD_DIM
B_QW = SW_Q_HEADS * HEAD_DIM
B_KVW = SW_KV_HEADS * HEAD_DIM
C_W = 2 * DIFF_HEADS * HEAD_DIM
_SPLIT_EDGES = (0, A_W, 2 * A_W, 3 * A_W, 3 * A_W + B_QW, 3 * A_W + B_QW + B_KVW,
                3 * A_W + B_QW + 2 * B_KVW, 3 * A_W + B_QW + 2 * B_KVW + C_W,
                3 * A_W + B_QW + 2 * B_KVW + 2 * C_W, 3 * A_W + B_QW + 2 * B_KVW + 3 * C_W, None)

TOKEN_TILE = 512
FFN_TILE = 1024
FFN_SUB_ROWS = 256
NA_ROWS_PER_STEP = 4
SW_Q_TILE = 256
DIFF_Q_TILE = 256
DIFF_KV_TILE = 512
DIFF_Q_SUBTILES = 4
DIFF_PIECE_ROWS = 16
DIFF_WARM_ROWS = 128
DIFF_REF_SLACK = 40.0
SUBLANES = 8
HEAD_PIPELINE_DEPTH = 4


def _rmsnorm(x, g):
    return x * lax.rsqrt(jnp.mean(x * x, axis=-1, keepdims=True) + EPS) * g


def _dot(a, b):
    return jnp.dot(a, b, preferred_element_type=F32)


def _dot_nt(a, b):
    return lax.dot_general(a, b, (((1,), (1,)), ((), ())), preferred_element_type=F32)


def _resident(shape):
    return pl.BlockSpec(shape, lambda *_: (0,) * len(shape), pipeline_mode=pl.Buffered(1))


def _low_half_mask():
    return lax.broadcasted_iota(jnp.int32, (1, LANES), 1) < HEAD_DIM


def _fold8(x, op):
    return op(x.reshape(x.shape[0] // SUBLANES, SUBLANES, x.shape[1]), axis=0)


def _head_pipeline(heads, scores, attend, depth=HEAD_PIPELINE_DEPTH):
    heads = list(heads)
    ready = [scores(h) for h in heads[:depth]]
    out = []
    for idx, head in enumerate(heads):
        if idx + depth < len(heads):
            ready.append(scores(heads[idx + depth]))
        out.append(attend(head, ready.pop(0)))
    return out


def _ffn_kernel(x_ref, pre_g_ref, wgu_ref, wdown_ref, post_g_ref, o_ref, *, f_chunks, d_ff):
    for r0 in range(0, x_ref.shape[0], FFN_SUB_ROWS):
        rows = slice(r0, r0 + FFN_SUB_ROWS)
        x = x_ref[rows, :]
        xn = _rmsnorm(x, pre_g_ref[...]).astype(BF16)
        acc = None
        for c0, c1 in f_chunks:
            gate = _dot(xn, wgu_ref[:, c0:c1])
            up = _dot(xn, wgu_ref[:, d_ff + c0:d_ff + c1])
            act = (gate * jax.nn.sigmoid(gate) * up).astype(BF16)
            part = _dot(act, wdown_ref[c0:c1, :])
            acc = part if acc is None else acc + part
        o_ref[rows, :] = x + 0.5 * _rmsnorm(acc, post_g_ref[...])


def _ffn(h, pre_g, w_gu, w_down, post_g):
    t, d = h.shape
    d_ff = w_down.shape[0]
    step = 1024
    f_chunks = tuple((c, min(c + step, d_ff)) for c in range(0, d_ff, step))
    row = pl.BlockSpec((FFN_TILE, d), lambda i: (i, 0))
    return pl.pallas_call(
        functools.partial(_ffn_kernel, f_chunks=f_chunks, d_ff=d_ff),
        grid=(t // FFN_TILE,),
        in_specs=[row, _resident((1, d)), _resident(w_gu.shape), _resident(w_down.shape),
                  _resident((1, d))],
        out_specs=row,
        out_shape=jax.ShapeDtypeStruct((t, d), F32),
        compiler_params=pltpu.CompilerParams(
            dimension_semantics=("arbitrary",), vmem_limit_bytes=48 * MIB),
        name="ffn",
    )(h, pre_g, w_gu, w_down, post_g)


def _rope(t, cos, sin_lo, sin_hi):
    out = []
    for c in range(t.shape[1] // LANES):
        tc = t[:, c * LANES:(c + 1) * LANES]
        fwd = pltpu.roll(tc, LANES - HEAD_DIM // 2, axis=1)
        back = pltpu.roll(tc, HEAD_DIM // 2, axis=1)
        out.append(tc * cos + fwd * sin_lo + back * sin_hi)
    return jnp.concatenate(out, axis=1)


def _proj_kernel(h_ref, g_ref, wqk_ref, wvt_ref, cos_ref, sin_lo_ref, sin_hi_ref,
                 u_ref, qa_ref, ka_ref, qb_ref, kb_ref, qc_ref, kc_ref, vat_ref, vbt_ref, vct_ref):
    scale = HEAD_DIM ** -0.5 * LOG2E
    for r0 in range(0, h_ref.shape[0], FFN_SUB_ROWS):
        rows = slice(r0, r0 + FFN_SUB_ROWS)
        un = _rmsnorm(h_ref[rows, :], g_ref[...]).astype(BF16)
        u_ref[rows, :] = un
        cos, sin_lo, sin_hi = cos_ref[rows, :], sin_lo_ref[rows, :], sin_hi_ref[rows, :]
        col = 0
        for out_ref, rotary, scaled in ((qa_ref, False, True), (ka_ref, False, False),
                                        (qb_ref, True, True), (kb_ref, True, False),
                                        (qc_ref, True, True), (kc_ref, True, False)):
            width = out_ref.shape[1]
            t = _dot(un, wqk_ref[:, col:col + width])
            col += width
            if rotary:
                t = _rope(t, cos, sin_lo, sin_hi)
            out_ref[rows, :] = (t * scale if scaled else t).astype(BF16)
        row = 0
        for vt_ref in (vat_ref, vbt_ref, vct_ref):
            width = vt_ref.shape[0]
            vt_ref[:, rows] = _dot_nt(wvt_ref[row:row + width, :], un).astype(BF16)
            row += width


def _proj(h, g, w_qk, w_vt, cos, sin_lo, sin_hi):
    t, d = h.shape
    s = cos.shape[0]
    tiles_per_seq = s // TOKEN_TILE
    widths = (d, A_W, A_W, B_QW, B_KVW, C_W, C_W)
    v_widths = (A_W, B_KVW, C_W)
    rows = lambda w: pl.BlockSpec((TOKEN_TILE, w), lambda i: (i, 0))
    chans = lambda w: pl.BlockSpec((None, w, TOKEN_TILE),
                                   lambda i: (i // tiles_per_seq, 0, i % tiles_per_seq))
    table = pl.BlockSpec((TOKEN_TILE, LANES), lambda i: (i % tiles_per_seq, 0))
    return pl.pallas_call(
        _proj_kernel,
        grid=(t // TOKEN_TILE,),
        in_specs=[rows(d), _resident((1, d)), _resident(w_qk.shape), _resident(w_vt.shape),
                  table, table, table],
        out_specs=[rows(w) for w in widths] + [chans(w) for w in v_widths],
        out_shape=([jax.ShapeDtypeStruct((t, w), BF16) for w in widths]
                   + [jax.ShapeDtypeStruct((t // s, w, s), BF16) for w in v_widths]),
        compiler_params=pltpu.CompilerParams(
            dimension_semantics=("arbitrary",), vmem_limit_bytes=40 * MIB),
        name="proj",
    )(h, g, w_qk, w_vt, cos, sin_lo, sin_hi)


def _na_kernel(q_ref, kp_ref, kc_ref, kn_ref, vtp_ref, vtc_ref, vtn_ref, bias_ref, o_ref):
    blk = q_ref.shape[0]
    k_refs = (kp_ref, kc_ref, kn_ref)
    vt_refs = (vtp_ref, vtc_ref, vtn_ref)
    low = _low_half_mask()
    top = lax.broadcasted_iota(jnp.int32, (LANES, 1), 0) < HEAD_DIM
    lanes = lambda head: slice(head // 2 * LANES, (head // 2 + 1) * LANES)

    def scores(head):
        qc = q_ref[:, lanes(head)]
        qz = jnp.where(low if head % 2 == 0 else ~low, qc, jnp.zeros_like(qc))
        return [_dot_nt(k_refs[w][:, lanes(head)], qz) + bias_ref[head, w * blk:(w + 1) * blk, :]
                for w in range(3)]

    def attend(head, s):
        m = jnp.max(jnp.maximum(jnp.maximum(s[0], s[1]), s[2]), axis=0, keepdims=True)
        l8, o = None, None
        for w in range(3):
            p = jnp.exp2(s[w] - m)
            ps = _fold8(p, jnp.sum)
            l8 = ps if l8 is None else l8 + ps
            pv = _dot(vt_refs[w][lanes(head), :], p.astype(BF16))
            o = pv if o is None else o + pv
        return o / jnp.sum(l8, axis=0, keepdims=True)

    res = _head_pipeline(range(NA_HEADS), scores, attend)
    for c in range(A_W // LANES):
        o_ref[:, c * LANES:(c + 1) * LANES] = (
            jnp.where(top, res[2 * c], res[2 * c + 1]).T.astype(o_ref.dtype))


def _na_bias_table(rpb, grid_rows):
    n = NA_ROWS_PER_STEP
    c = np.arange(GRID_W)
    dc = np.clip(c[:, None] - c[None, :] + (NA_COL_K - 1), 0, 2 * NA_COL_K - 2)
    col_start = np.clip(c - NA_COL_K // 2, 0, GRID_W - NA_COL_K)
    col_ok = (c[:, None] >= col_start[None, :]) & (c[:, None] < col_start[None, :] + NA_COL_K)
    one_hot = jnp.asarray(dc[None] == np.arange(2 * NA_COL_K - 1)[:, None, None], F32)
    expanded = jnp.einsum("hdj,jkq->hdkq", rpb.astype(F32), one_hot,
                          precision=lax.Precision.HIGHEST)
    by_offset = jnp.where(col_ok, expanded * LOG2E, NEG)
    masked = jnp.full_like(by_offset[:, 0], NEG)
    tables = []
    for first_row in (0, n, grid_rows - n):
        key_rows = []
        for kr in range(3 * n):
            blocks = []
            for qr in range(n):
                q_abs, k_abs = first_row + qr, first_row - n + kr
                row_start = min(max(q_abs - NA_ROW_K // 2, 0), grid_rows - NA_ROW_K)
                in_window = row_start <= k_abs < row_start + NA_ROW_K
                blocks.append(by_offset[:, k_abs - q_abs + NA_ROW_K - 1] if in_window else masked)
            key_rows.append(jnp.concatenate(blocks, axis=-1))
        tables.append(jnp.concatenate(key_rows, axis=1))
    return jnp.stack(tables)


def _na(q, k, vt, bias):
    b, s, w = q.shape
    blk = NA_ROWS_PER_STEP * GRID_W
    n_blk = s // blk
    clamp = lambda j: jnp.clip(j, 0, n_blk - 1)
    tok = lambda shift: pl.BlockSpec((None, blk, w), lambda bi, j: (bi, clamp(j + shift), 0))
    chan = lambda shift: pl.BlockSpec((None, w, blk), lambda bi, j: (bi, 0, clamp(j + shift)))
    table = pl.BlockSpec(
        (None,) + bias.shape[1:],
        lambda bi, j: (jnp.where(j == 0, 0, jnp.where(j == n_blk - 1, 2, 1)), 0, 0, 0))
    return pl.pallas_call(
        _na_kernel,
        grid=(b, n_blk),
        in_specs=[tok(0), tok(-1), tok(0), tok(1), chan(-1), chan(0), chan(1), table],
        out_specs=tok(0),
        out_shape=jax.ShapeDtypeStruct((b, s, w), BF16),
        compiler_params=pltpu.CompilerParams(
            dimension_semantics=("arbitrary", "arbitrary"), vmem_limit_bytes=40 * MIB),
        name="neighbourhood_attn",
    )(q, k, k, k, vt, vt, vt, bias)


def _sw_kernel(sink_ref, q_ref, kp_ref, kc_ref, kn_ref, vp_ref, vc_ref, vn_ref, o_ref):
    i = pl.program_id(1)
    tq = q_ref.shape[0]
    win = SW_WINDOW
    k = jnp.concatenate([kp_ref[...], kc_ref[...], kn_ref[...]], axis=0)
    vt = jnp.concatenate([vp_ref[...], vc_ref[...], vn_ref[...]], axis=1)
    n_kblk = k.shape[0] // win
    key_row = lax.broadcasted_iota(jnp.int32, (win, win), 0)
    query_col = lax.broadcasted_iota(jnp.int32, (win, win), 1)
    band = {0: jnp.where(key_row >= query_col, 0.0, NEG), 1: None,
            2: jnp.where(key_row <= query_col, 0.0, NEG)}
    outside = {0: jnp.where(i == 0, NEG, 0.0),
               n_kblk - 1: jnp.where(i == pl.num_programs(1) - 1, NEG, 0.0)}
    masks = {}
    for c in range(tq // win):
        for d in range(3):
            a = c + d
            mask = band[d]
            if a in outside:
                mask = outside[a] if mask is None else mask + outside[a]
            masks[a, c] = mask
    low = _low_half_mask()
    group = SW_Q_HEADS // SW_KV_HEADS

    def scores(head):
        kv_head = head // group
        qc = q_ref[:, head // 2 * LANES:(head // 2 + 1) * LANES]
        src = qc if head % 2 == kv_head else pltpu.roll(qc, HEAD_DIM, axis=1)
        qz = jnp.where(low if kv_head == 0 else ~low, src, jnp.zeros_like(src))
        return _dot_nt(k, qz)

    def attend(head, s):
        kv_head = head // group
        sink = sink_ref[0, head]
        p_cols, l_cols = [], []
        for c in range(tq // win):
            blocks = []
            for a in range(c, c + 3):
                blk = s[a * win:(a + 1) * win, c * win:(c + 1) * win]
                blocks.append(blk if masks[a, c] is None else blk + masks[a, c])
            m = jnp.maximum(jnp.max(jnp.maximum(jnp.maximum(blocks[0], blocks[1]), blocks[2]),
                                    axis=0, keepdims=True), sink)
            l8, packed = None, []
            for blk in blocks:
                p = jnp.exp2(blk - m)
                ps = _fold8(p, jnp.sum)
                l8 = ps if l8 is None else l8 + ps
                packed.append(p.astype(BF16))
            l_cols.append(jnp.sum(l8, axis=0, keepdims=True) + jnp.exp2(sink - m))
            empty = jnp.zeros((win, win), BF16)
            p_cols.append(jnp.concatenate(
                [empty] * c + packed + [empty] * (n_kblk - c - 3), axis=0))
        p_all = jnp.concatenate(p_cols, axis=1)
        o = _dot(vt, p_all) / jnp.concatenate(l_cols, axis=1)
        return o[kv_head * HEAD_DIM:(kv_head + 1) * HEAD_DIM]

    res = _head_pipeline(range(SW_Q_HEADS), scores, attend)
    for c in range(B_QW // LANES):
        o_ref[:, c * LANES:(c + 1) * LANES] = (
            jnp.concatenate(res[2 * c:2 * c + 2], axis=0).T.astype(o_ref.dtype))


def _sw(q, k, vt, sink):
    b, s, w = q.shape
    kvw = k.shape[-1]
    tq = SW_Q_TILE
    per = tq // SW_WINDOW
    n_blk = s // SW_WINDOW
    prev = lambda i: jnp.maximum(per * i - 1, 0)
    nxt = lambda i: jnp.minimum(per * (i + 1), n_blk - 1)
    q_spec = pl.BlockSpec((None, tq, w), lambda bi, i: (bi, i, 0))
    k_specs = [pl.BlockSpec((None, SW_WINDOW, kvw), lambda bi, i: (bi, prev(i), 0)),
               pl.BlockSpec((None, tq, kvw), lambda bi, i: (bi, i, 0)),
               pl.BlockSpec((None, SW_WINDOW, kvw), lambda bi, i: (bi, nxt(i), 0))]
    vt_specs = [pl.BlockSpec((None, kvw, SW_WINDOW), lambda bi, i: (bi, 0, prev(i))),
                pl.BlockSpec((None, kvw, tq), lambda bi, i: (bi, 0, i)),
                pl.BlockSpec((None, kvw, SW_WINDOW), lambda bi, i: (bi, 0, nxt(i)))]
    sink_spec = pl.BlockSpec(memory_space=pltpu.SMEM)
    return pl.pallas_call(
        _sw_kernel,
        grid=(b, s // tq),
        in_specs=[sink_spec, q_spec] + k_specs + vt_specs,
        out_specs=q_spec,
        out_shape=jax.ShapeDtypeStruct((b, s, w), BF16),
        compiler_params=pltpu.CompilerParams(dimension_semantics=("arbitrary", "arbitrary")),
        name="sliding_window_attn",
    )(sink, q, k, k, k, vt, vt, vt)


def _diff_kernel(q_ref, k_ref, vt_ref, lq1_ref, lk1_ref, lq2_ref, lk2_ref, g_ref, o_ref,
                 m_ref, *scratch, lam_init):
    tq = DIFF_Q_TILE
    n_sub = q_ref.shape[0] // tq
    per = len(scratch) // n_sub
    subs = [scratch[i * per:(i + 1) * per] for i in range(n_sub)]
    kv_tile = subs[0][2].shape[0]
    n_kv = k_ref.shape[0] // kv_tile
    cols = 2 * tq
    low = _low_half_mask()

    fold8 = _fold8

    def col_max(x8):
        return jnp.max(x8, axis=0, keepdims=True)

    lam = (jnp.exp(jnp.sum(lq1_ref[...] * lk1_ref[...]))
           - jnp.exp(jnp.sum(lq2_ref[...] * lk2_ref[...])) + lam_init)

    def finalize(sub):
        l_ref, acc_ref = subs[sub][:2]
        o = acc_ref[...] / l_ref[...]
        o = o[:, :tq] - lam * o[:, tq:]
        o = o * lax.rsqrt(jnp.mean(o * o, axis=0, keepdims=True) + EPS) * g_ref[...]
        o_ref[sub * tq:(sub + 1) * tq, :] = (o * (1.0 - lam_init)).T.astype(o_ref.dtype)

    def fast_sweep(sub):
        l_ref, acc_ref, p0_ref, p1_ref, qz_ref = subs[sub]
        p_bufs = (p0_ref, p1_ref)
        q = q_ref[sub * tq:(sub + 1) * tq, :]
        zero = jnp.zeros_like(q)
        qz_ref[0:tq, :] = jnp.where(low, q, zero)
        qz_ref[tq:, :] = jnp.where(low, zero, q)
        ref = None
        unsafe = jnp.zeros((1, cols), F32)
        l8 = None
        beta = None
        pending = None

        def value_product(tile, scale):
            pv = _dot(vt_ref[:, tile * kv_tile:(tile + 1) * kv_tile], p_bufs[tile % 2][...])
            acc_ref[...] = pv if scale is None else scale * acc_ref[...] + pv

        for t in range(n_kv):
            tile_max8 = None
            tile_sum8 = None
            s = _dot_nt(k_ref[t * kv_tile:(t + 1) * kv_tile, :], qz_ref[...])
            if ref is None:
                ref = col_max(fold8(s[0:DIFF_WARM_ROWS], jnp.max))
            if pending is not None:
                value_product(*pending)
            for r in range(0, kv_tile, DIFF_PIECE_ROWS):
                piece = s[r:r + DIFF_PIECE_ROWS]
                mx = fold8(piece, jnp.max)
                tile_max8 = mx if tile_max8 is None else jnp.maximum(tile_max8, mx)
                p = jnp.exp2(piece - ref)
                ps = fold8(p, jnp.sum)
                tile_sum8 = ps if tile_sum8 is None else tile_sum8 + ps
                p_bufs[t % 2][r:r + DIFF_PIECE_ROWS, :] = p.astype(BF16)
            pending = (t, beta)
            l8 = tile_sum8 if beta is None else beta * l8 + tile_sum8
            tile_max = col_max(tile_max8)
            unsafe = jnp.maximum(unsafe, jnp.where(tile_max - ref > DIFF_REF_SLACK, 1.0, 0.0))
            new_ref = jnp.maximum(ref, tile_max)
            beta = jnp.exp2(ref - new_ref)
            ref = new_ref
        value_product(*pending)
        l_fast = jnp.sum(l8, axis=0, keepdims=True)
        l_ref[...] = l_fast
        finite = jnp.isfinite(jnp.sum(acc_ref[...], axis=0, keepdims=True) + l_fast)
        return jnp.maximum(unsafe, jnp.where(finite, 0.0, 1.0))

    def exact_sweep(sub):
        l_ref, acc_ref, _, _, qz_ref = subs[sub]
        m_ref[...] = jnp.full(m_ref.shape, -jnp.inf, F32)
        l_ref[...] = jnp.zeros(l_ref.shape, F32)
        acc_ref[...] = jnp.zeros(acc_ref.shape, F32)

        def tile_step(j, carry):
            rows = pl.ds(pl.multiple_of(j * kv_tile, kv_tile), kv_tile)
            s = _dot_nt(k_ref[rows, :], qz_ref[...])
            m_old = m_ref[...]
            m_new = jnp.maximum(m_old, jnp.max(s, axis=0, keepdims=True))
            alpha = jnp.exp2(m_old - m_new)
            p = jnp.exp2(s - m_new)
            m_ref[...] = m_new
            l_ref[...] = alpha * l_ref[...] + jnp.sum(p, axis=0, keepdims=True)
            acc_ref[...] = alpha * acc_ref[...] + _dot(vt_ref[:, rows], p.astype(BF16))
            return carry

        lax.fori_loop(0, n_kv, tile_step, 0)

    unsafe = None
    for sub in range(n_sub):
        flag = fast_sweep(sub)
        unsafe = flag if unsafe is None else jnp.maximum(unsafe, flag)
        finalize(sub)

    @pl.when(jnp.max(unsafe) > 0.0)
    def _():
        for sub in range(n_sub):
            exact_sweep(sub)
            finalize(sub)


def _diff(q, k, vt, lq1, lk1, lq2, lk2, subln_g, lam_init):
    b, s, w = q.shape
    tq = DIFF_Q_TILE
    dv = 2 * HEAD_DIM
    heads = w // LANES
    kv_tile = min(DIFF_KV_TILE, s)
    n_sub = DIFF_Q_SUBTILES
    q_spec = pl.BlockSpec((None, n_sub * tq, LANES), lambda bi, h, i: (bi, i, h))
    k_spec = pl.BlockSpec((None, s, LANES), lambda bi, h, i: (bi, 0, h))
    vt_spec = pl.BlockSpec((None, dv, s), lambda bi, h, i: (bi, h, 0))
    vec = lambda n: pl.BlockSpec((1, n), lambda bi, h, i: (0, 0))
    sub_scratch = [pltpu.VMEM((1, 2 * tq), F32), pltpu.VMEM((dv, 2 * tq), F32),
                   pltpu.VMEM((kv_tile, 2 * tq), BF16), pltpu.VMEM((kv_tile, 2 * tq), BF16),
                   pltpu.VMEM((2 * tq, LANES), BF16)]
    return pl.pallas_call(
        functools.partial(_diff_kernel, lam_init=lam_init),
        grid=(b, heads, s // (n_sub * tq)),
        in_specs=[q_spec, k_spec, vt_spec, vec(HEAD_DIM), vec(HEAD_DIM), vec(HEAD_DIM),
                  vec(HEAD_DIM), pl.BlockSpec((dv, 1), lambda bi, h, i: (0, 0))],
        out_specs=q_spec,
        out_shape=jax.ShapeDtypeStruct((b, s, w), BF16),
        scratch_shapes=[pltpu.VMEM((1, 2 * tq), F32)] + sub_scratch * n_sub,
        compiler_params=pltpu.CompilerParams(
            dimension_semantics=("arbitrary", "arbitrary", "arbitrary"),
            vmem_limit_bytes=32 * MIB),
        name="differential_attn",
    )(q, k, vt, lq1, lk1, lq2, lk2, subln_g)


def _merge_kernel(h_ref, u_ref, oa_ref, ob_ref, oc_ref, wg_ref, bg_ref, wb_ref, wo_ref, g_ref,
                  o_ref):
    d = h_ref.shape[1]
    for r0 in range(0, h_ref.shape[0], FFN_SUB_ROWS):
        rows = slice(r0, r0 + FFN_SUB_ROWS)
        u = u_ref[rows, :]
        merged = None
        for n, br_ref in enumerate((oa_ref, ob_ref, oc_ref)):
            gate = jax.nn.sigmoid(_dot(u, wg_ref[:, n * d:(n + 1) * d]) + bg_ref[n:n + 1, :])
            term = gate * _dot(br_ref[rows, :], wb_ref[n])
            merged = term if merged is None else merged + term
        y = _dot(merged.astype(BF16), wo_ref[...])
        o_ref[rows, :] = h_ref[rows, :] + _rmsnorm(y, g_ref[...])


def _merge(h, u, oa, ob, oc, w_gate, b_gate, w_branch, w_out, post_g):
    t, d = h.shape
    rows = lambda w: pl.BlockSpec((TOKEN_TILE, w), lambda i: (i, 0))
    return pl.pallas_call(
        _merge_kernel,
        grid=(t // TOKEN_TILE,),
        in_specs=[rows(d), rows(d), rows(oa.shape[1]), rows(ob.shape[1]), rows(oc.shape[1]),
                  _resident(w_gate.shape), _resident(b_gate.shape), _resident(w_branch.shape),
                  _resident(w_out.shape), _resident((1, d))],
        out_specs=rows(d),
        out_shape=jax.ShapeDtypeStruct((t, d), F32),
        compiler_params=pltpu.CompilerParams(
            dimension_semantics=("arbitrary",), vmem_limit_bytes=40 * MIB),
        name="gated_merge",
    )(h, u, oa, ob, oc, w_gate, b_gate, w_branch, w_out, post_g)


def _rope_tables(s):
    pos = jnp.arange(s, dtype=F32)
    inv = ROPE_THETA ** (-jnp.arange(0, HEAD_DIM, 2, dtype=F32) / HEAD_DIM)
    ang = pos[:, None] * inv[None, :]
    cos = jnp.tile(jnp.cos(ang), (1, 2 * LANES // HEAD_DIM))
    sin = jnp.tile(jnp.sin(ang), (1, 2 * LANES // HEAD_DIM))
    first_half = (jnp.arange(LANES) % HEAD_DIM) < HEAD_DIM // 2
    return cos, jnp.where(first_half, -sin, 0.0), jnp.where(first_half, 0.0, sin)


def kernel(x, ffn1_pre_g, ffn1_w_gu, ffn1_w_down, ffn1_post_g, mix_pre_g, w_in, na_rpb, sw_sink, diff_lambda_q1, diff_lambda_k1, diff_lambda_q2, diff_lambda_k2, diff_subln_g, w_branch, b_gate, w_out, mix_post_g, ffn2_pre_g, ffn2_w_gu, ffn2_w_down, ffn2_post_g):
    b, s, d = x.shape
    depth = w_in.shape[0]
    assert s % (NA_ROWS_PER_STEP * GRID_W) == 0 and s % TOKEN_TILE == 0 and s >= NA_ROW_K * GRID_W
    cos, sin_lo, sin_hi = _rope_tables(s)
    row = lambda a: a.reshape(1, -1).astype(F32)
    h = x.reshape(b * s, d)
    for l in range(depth):
        h = _ffn(h, row(ffn1_pre_g[l]), ffn1_w_gu[l].astype(BF16), ffn1_w_down[l].astype(BF16),
                 row(ffn1_post_g[l]))

        w_l = w_in[l]
        qa_w, ka_w, va_w, qb_w, kb_w, vb_w, qc_w, kc_w, vc_w, gate_w = (
            w_l[:, c0:c1] for c0, c1 in zip(_SPLIT_EDGES[:-1], _SPLIT_EDGES[1:]))
        w_qk = jnp.concatenate([qa_w, ka_w, qb_w, kb_w, qc_w, kc_w], axis=1).astype(BF16)
        w_vt = jnp.concatenate([va_w, vb_w, vc_w], axis=1).T.astype(BF16)
        u, qa, ka, qb, kb, qc, kc, vat, vbt, vct = _proj(
            h, row(mix_pre_g[l]), w_qk, w_vt, cos, sin_lo, sin_hi)
        seq = lambda a: a.reshape(b, s, a.shape[-1])
        oa = _na(seq(qa), seq(ka), vat, _na_bias_table(na_rpb[l], s // GRID_W))
        ob = _sw(seq(qb), seq(kb), vbt, sw_sink[l].reshape(1, -1).astype(F32) * LOG2E)
        lam_init = 0.8 - 0.6 * math.exp(-0.3 * l)
        oc = _diff(seq(qc), seq(kc), vct, row(diff_lambda_q1[l]), row(diff_lambda_k1[l]),
                   row(diff_lambda_q2[l]), row(diff_lambda_k2[l]),
                   diff_subln_g[l].reshape(-1, 1).astype(F32), lam_init)
        flat = lambda a: a.reshape(b * s, a.shape[-1])
        h = _merge(h, u, flat(oa), flat(ob), flat(oc), gate_w.astype(BF16), b_gate[l].astype(F32),
                   w_branch[l].astype(BF16), w_out[l].astype(BF16), row(mix_post_g[l]))

        h = _ffn(h, row(ffn2_pre_g[l]), ffn2_w_gu[l].astype(BF16), ffn2_w_down[l].astype(BF16),
                 row(ffn2_post_g[l]))
    return h.reshape(b, s, d)
```

```python
import functools
import math

import jax
import jax.numpy as jnp
import numpy as np
from jax import lax
from jax.experimental import pallas as pl
from jax.experimental.pallas import tpu as pltpu

F32 = jnp.float32
BF16 = jnp.bfloat16

HEAD_DIM = 64
GRID_W = 64
NA_HEADS = 8
NA_ROW_K = 8
NA_COL_K = 16
SW_Q_HEADS = 8
SW_KV_HEADS = 2
SW_WINDOW = 128
DIFF_HEADS = 4
N_BRANCH = 3
ROPE_THETA = 10000.0
EPS = 1e-6
LOG2E = math.log2(math.e)
NEG = -1e30
LANES = 128
SUBLANES = 8
MIB = 1024 * 1024

A_W = NA_HEADS * HEAD_DIM
B_QW = SW_Q_HEADS * HEAD_DIM
B_KVW = SW_KV_HEADS * HEAD_DIM
C_W = 2 * DIFF_HEADS * HEAD_DIM
_SPLIT_EDGES = (0, A_W, 2 * A_W, 3 * A_W, 3 * A_W + B_QW, 3 * A_W + B_QW + B_KVW,
                3 * A_W + B_QW + 2 * B_KVW, 3 * A_W + B_QW + 2 * B_KVW + C_W,
                3 * A_W + B_QW + 2 * B_KVW + 2 * C_W, 3 * A_W + B_QW + 2 * B_KVW + 3 * C_W, None)

TOKEN_TILE = 1024
FFN_TILE = 1024
SUB_ROWS = 256
NA_ROWS_PER_STEP = 4
SW_Q_TILE = 256
DIFF_Q_TILE = 256
DIFF_KV_TILE = 512
DIFF_Q_SUBTILES = 4
DIFF_PIECE_ROWS = 16
DIFF_WARM_ROWS = 128
DIFF_REF_SLACK = 40.0
HEAD_PIPELINE_DEPTH = 4


def _rmsnorm(x, g):
    return x * lax.rsqrt(jnp.mean(x * x, axis=-1, keepdims=True) + EPS) * g


def _dot(a, b):
    return jnp.dot(a, b, preferred_element_type=F32)


def _dot_nt(a, b):
    return lax.dot_general(a, b, (((1,), (1,)), ((), ())), preferred_element_type=F32)


def _resident(shape):
    return pl.BlockSpec(shape, lambda *_: (0,) * len(shape), pipeline_mode=pl.Buffered(1))


def _low_half_mask():
    return lax.broadcasted_iota(jnp.int32, (1, LANES), 1) < HEAD_DIM


def _fold8(x, op):
    return op(x.reshape(x.shape[0] // SUBLANES, SUBLANES, x.shape[1]), axis=0)


def _col_max(x8):
    return jnp.max(x8, axis=0, keepdims=True)


def _head_pipeline(heads, scores, attend, depth=HEAD_PIPELINE_DEPTH):
    heads = list(heads)
    ready = [scores(h) for h in heads[:depth]]
    out = []
    for idx, head in enumerate(heads):
        if idx + depth < len(heads):
            ready.append(scores(heads[idx + depth]))
        out.append(attend(head, ready.pop(0)))
    return out


def _ffn_kernel(x_ref, pre_g_ref, wgu_ref, wdown_ref, post_g_ref, o_ref, *, f_chunks, d_ff):
    for r0 in range(0, x_ref.shape[0], SUB_ROWS):
        rows = slice(r0, r0 + SUB_ROWS)
        x = x_ref[rows, :]
        xn = _rmsnorm(x, pre_g_ref[...]).astype(BF16)
        acc = None
        for c0, c1 in f_chunks:
            gate = _dot(xn, wgu_ref[:, c0:c1])
            up = _dot(xn, wgu_ref[:, d_ff + c0:d_ff + c1])
            act = (gate * jax.nn.sigmoid(gate) * up).astype(BF16)
            part = _dot(act, wdown_ref[c0:c1, :])
            acc = part if acc is None else acc + part
        o_ref[rows, :] = x + 0.5 * _rmsnorm(acc, post_g_ref[...])


def _ffn(h, pre_g, w_gu, w_down, post_g):
    t, d = h.shape
    d_ff = w_down.shape[0]
    step = 1024
    f_chunks = tuple((c, min(c + step, d_ff)) for c in range(0, d_ff, step))
    row = pl.BlockSpec((FFN_TILE, d), lambda i: (i, 0))
    return pl.pallas_call(
        functools.partial(_ffn_kernel, f_chunks=f_chunks, d_ff=d_ff),
        grid=(t // FFN_TILE,),
        in_specs=[row, _resident((1, d)), _resident(w_gu.shape), _resident(w_down.shape),
                  _resident((1, d))],
        out_specs=row,
        out_shape=jax.ShapeDtypeStruct((t, d), F32),
        compiler_params=pltpu.CompilerParams(
            dimension_semantics=("arbitrary",), vmem_limit_bytes=48 * MIB),
        name="ffn",
    )(h, pre_g, w_gu, w_down, post_g)


def _rope(t, cos, sin_lo, sin_hi):
    out = []
    for c in range(t.shape[1] // LANES):
        tc = t[:, c * LANES:(c + 1) * LANES]
        fwd = pltpu.roll(tc, LANES - HEAD_DIM // 2, axis=1)
        back = pltpu.roll(tc, HEAD_DIM // 2, axis=1)
        out.append(tc * cos + fwd * sin_lo + back * sin_hi)
    return jnp.concatenate(out, axis=1)


def _proj_kernel(h_ref, g_ref, wqk_ref, wvt_ref, cos_ref, sin_lo_ref, sin_hi_ref,
                 u_ref, qa_ref, ka_ref, qb_ref, kb_ref, qc_ref, kc_ref, vat_ref, vbt_ref, vct_ref):
    scale = HEAD_DIM ** -0.5 * LOG2E
    for r0 in range(0, h_ref.shape[0], SUB_ROWS):
        rows = slice(r0, r0 + SUB_ROWS)
        un = _rmsnorm(h_ref[rows, :], g_ref[...]).astype(BF16)
        u_ref[rows, :] = un
        cos, sin_lo, sin_hi = cos_ref[rows, :], sin_lo_ref[rows, :], sin_hi_ref[rows, :]
        col = 0
        for out_ref, rotary, scaled in ((qa_ref, False, True), (ka_ref, False, False),
                                        (qb_ref, True, True), (kb_ref, True, False),
                                        (qc_ref, True, True), (kc_ref, True, False)):
            width = out_ref.shape[1]
            t = _dot(un, wqk_ref[:, col:col + width])
            col += width
            if rotary:
                t = _rope(t, cos, sin_lo, sin_hi)
            out_ref[rows, :] = (t * scale if scaled else t).astype(BF16)
        row = 0
        for vt_ref in (vat_ref, vbt_ref, vct_ref):
            width = vt_ref.shape[0]
            vt_ref[:, rows] = _dot_nt(wvt_ref[row:row + width, :], un).astype(BF16)
            row += width


def _proj(h, g, w_qk, w_vt, cos, sin_lo, sin_hi):
    t, d = h.shape
    s = cos.shape[0]
    tiles_per_seq = s // TOKEN_TILE
    widths = (d, A_W, A_W, B_QW, B_KVW, C_W, C_W)
    v_widths = (A_W, B_KVW, C_W)
    rows = lambda w: pl.BlockSpec((TOKEN_TILE, w), lambda i: (i, 0))
    chans = lambda w: pl.BlockSpec((None, w, TOKEN_TILE),
                                   lambda i: (i // tiles_per_seq, 0, i % tiles_per_seq))
    table = pl.BlockSpec((TOKEN_TILE, LANES), lambda i: (i % tiles_per_seq, 0))
    return pl.pallas_call(
        _proj_kernel,
        grid=(t // TOKEN_TILE,),
        in_specs=[rows(d), _resident((1, d)), _resident(w_qk.shape), _resident(w_vt.shape),
                  table, table, table],
        out_specs=[rows(w) for w in widths] + [chans(w) for w in v_widths],
        out_shape=([jax.ShapeDtypeStruct((t, w), BF16) for w in widths]
                   + [jax.ShapeDtypeStruct((t // s, w, s), BF16) for w in v_widths]),
        compiler_params=pltpu.CompilerParams(
            dimension_semantics=("arbitrary",), vmem_limit_bytes=48 * MIB),
        name="proj",
    )(h, g, w_qk, w_vt, cos, sin_lo, sin_hi)


def _na_kernel(q_ref, kp_ref, kc_ref, kn_ref, vtp_ref, vtc_ref, vtn_ref, bias_ref, o_ref):
    blk = q_ref.shape[0]
    k_refs = (kp_ref, kc_ref, kn_ref)
    vt_refs = (vtp_ref, vtc_ref, vtn_ref)
    low = _low_half_mask()
    top = lax.broadcasted_iota(jnp.int32, (LANES, 1), 0) < HEAD_DIM
    lanes = lambda head: slice(head // 2 * LANES, (head // 2 + 1) * LANES)

    def scores(head):
        qc = q_ref[:, lanes(head)]
        qz = jnp.where(low if head % 2 == 0 else ~low, qc, jnp.zeros_like(qc))
        return [_dot_nt(k_refs[w][:, lanes(head)], qz) + bias_ref[head, w * blk:(w + 1) * blk, :]
                for w in range(3)]

    def attend(head, s):
        m = jnp.max(jnp.maximum(jnp.maximum(s[0], s[1]), s[2]), axis=0, keepdims=True)
        l8, o = None, None
        for w in range(3):
            p = jnp.exp2(s[w] - m)
            ps = _fold8(p, jnp.sum)
            l8 = ps if l8 is None else l8 + ps
            pv = _dot(vt_refs[w][lanes(head), :], p.astype(BF16))
            o = pv if o is None else o + pv
        return o / jnp.sum(l8, axis=0, keepdims=True)

    res = _head_pipeline(range(NA_HEADS), scores, attend)
    for c in range(A_W // LANES):
        o_ref[:, c * LANES:(c + 1) * LANES] = (
            jnp.where(top, res[2 * c], res[2 * c + 1]).T.astype(o_ref.dtype))


def _na_bias_table(rpb, grid_rows):
    n = NA_ROWS_PER_STEP
    c = np.arange(GRID_W)
    dc = np.clip(c[:, None] - c[None, :] + (NA_COL_K - 1), 0, 2 * NA_COL_K - 2)
    col_start = np.clip(c - NA_COL_K // 2, 0, GRID_W - NA_COL_K)
    col_ok = (c[:, None] >= col_start[None, :]) & (c[:, None] < col_start[None, :] + NA_COL_K)
    one_hot = jnp.asarray(dc[None] == np.arange(2 * NA_COL_K - 1)[:, None, None], F32)
    expanded = jnp.einsum("hdj,jkq->hdkq", rpb.astype(F32), one_hot,
                          precision=lax.Precision.HIGHEST)
    by_offset = jnp.where(col_ok, expanded * LOG2E, NEG)
    masked = jnp.full_like(by_offset[:, 0], NEG)
    tables = []
    for first_row in (0, n, grid_rows - n):
        key_rows = []
        for kr in range(3 * n):
            blocks = []
            for qr in range(n):
                q_abs, k_abs = first_row + qr, first_row - n + kr
                row_start = min(max(q_abs - NA_ROW_K // 2, 0), grid_rows - NA_ROW_K)
                in_window = row_start <= k_abs < row_start + NA_ROW_K
                blocks.append(by_offset[:, k_abs - q_abs + NA_ROW_K - 1] if in_window else masked)
            key_rows.append(jnp.concatenate(blocks, axis=-1))
        tables.append(jnp.concatenate(key_rows, axis=1))
    return jnp.stack(tables)


def _na(q, k, vt, bias):
    b, s, w = q.shape
    blk = NA_ROWS_PER_STEP * GRID_W
    n_blk = s // blk
    clamp = lambda j: jnp.clip(j, 0, n_blk - 1)
    tok = lambda shift: pl.BlockSpec((None, blk, w), lambda bi, j: (bi, clamp(j + shift), 0))
    chan = lambda shift: pl.BlockSpec((None, w, blk), lambda bi, j: (bi, 0, clamp(j + shift)))
    table = pl.BlockSpec(
        (None,) + bias.shape[1:],
        lambda bi, j: (jnp.where(j == 0, 0, jnp.where(j == n_blk - 1, 2, 1)), 0, 0, 0))
    return pl.pallas_call(
        _na_kernel,
        grid=(b, n_blk),
        in_specs=[tok(0), tok(-1), tok(0), tok(1), chan(-1), chan(0), chan(1), table],
        out_specs=tok(0),
        out_shape=jax.ShapeDtypeStruct((b, s, w), BF16),
        compiler_params=pltpu.CompilerParams(
            dimension_semantics=("arbitrary", "arbitrary"), vmem_limit_bytes=40 * MIB),
        name="neighbourhood_attn",
    )(q, k, k, k, vt, vt, vt, bias)


def _sw_kernel(sink_ref, q_ref, kp_ref, kc_ref, kn_ref, vp_ref, vc_ref, vn_ref, o_ref):
    i = pl.program_id(1)
    tq = q_ref.shape[0]
    win = SW_WINDOW
    k = jnp.concatenate([kp_ref[...], kc_ref[...], kn_ref[...]], axis=0)
    vt = jnp.concatenate([vp_ref[...], vc_ref[...], vn_ref[...]], axis=1)
    n_kblk = k.shape[0] // win
    key_row = lax.broadcasted_iota(jnp.int32, (win, win), 0)
    query_col = lax.broadcasted_iota(jnp.int32, (win, win), 1)
    band = {0: jnp.where(key_row >= query_col, 0.0, NEG), 1: None,
            2: jnp.where(key_row <= query_col, 0.0, NEG)}
    outside = {0: jnp.where(i == 0, NEG, 0.0),
               n_kblk - 1: jnp.where(i == pl.num_programs(1) - 1, NEG, 0.0)}
    masks = {}
    for c in range(tq // win):
        for d in range(3):
            a = c + d
            mask = band[d]
            if a in outside:
                mask = outside[a] if mask is None else mask + outside[a]
            masks[a, c] = mask
    low = _low_half_mask()
    group = SW_Q_HEADS // SW_KV_HEADS

    def scores(head):
        kv_head = head // group
        qc = q_ref[:, head // 2 * LANES:(head // 2 + 1) * LANES]
        src = qc if head % 2 == kv_head else pltpu.roll(qc, HEAD_DIM, axis=1)
        qz = jnp.where(low if kv_head == 0 else ~low, src, jnp.zeros_like(src))
        return _dot_nt(k, qz)

    def attend(head, s):
        kv_head = head // group
        sink = sink_ref[0, head]
        p_cols, l_cols = [], []
        for c in range(tq // win):
            blocks = []
            for a in range(c, c + 3):
                blk = s[a * win:(a + 1) * win, c * win:(c + 1) * win]
                blocks.append(blk if masks[a, c] is None else blk + masks[a, c])
            m = jnp.maximum(jnp.max(jnp.maximum(jnp.maximum(blocks[0], blocks[1]), blocks[2]),
                                    axis=0, keepdims=True), sink)
            l8, packed = None, []
            for blk in blocks:
                p = jnp.exp2(blk - m)
                ps = _fold8(p, jnp.sum)
                l8 = ps if l8 is None else l8 + ps
                packed.append(p.astype(BF16))
            l_cols.append(jnp.sum(l8, axis=0, keepdims=True) + jnp.exp2(sink - m))
            empty = jnp.zeros((win, win), BF16)
            p_cols.append(jnp.concatenate(
                [empty] * c + packed + [empty] * (n_kblk - c - 3), axis=0))
        p_all = jnp.concatenate(p_cols, axis=1)
        o = _dot(vt, p_all) / jnp.concatenate(l_cols, axis=1)
        return o[kv_head * HEAD_DIM:(kv_head + 1) * HEAD_DIM]

    res = _head_pipeline(range(SW_Q_HEADS), scores, attend)
    for c in range(B_QW // LANES):
        o_ref[:, c * LANES:(c + 1) * LANES] = (
            jnp.concatenate(res[2 * c:2 * c + 2], axis=0).T.astype(o_ref.dtype))


def _sw(q, k, vt, sink):
    b, s, w = q.shape
    kvw = k.shape[-1]
    tq = SW_Q_TILE
    per = tq // SW_WINDOW
    n_blk = s // SW_WINDOW
    prev = lambda i: jnp.maximum(per * i - 1, 0)
    nxt = lambda i: jnp.minimum(per * (i + 1), n_blk - 1)
    q_spec = pl.BlockSpec((None, tq, w), lambda bi, i: (bi, i, 0))
    k_specs = [pl.BlockSpec((None, SW_WINDOW, kvw), lambda bi, i: (bi, prev(i), 0)),
               pl.BlockSpec((None, tq, kvw), lambda bi, i: (bi, i, 0)),
               pl.BlockSpec((None, SW_WINDOW, kvw), lambda bi, i: (bi, nxt(i), 0))]
    vt_specs = [pl.BlockSpec((None, kvw, SW_WINDOW), lambda bi, i: (bi, 0, prev(i))),
                pl.BlockSpec((None, kvw, tq), lambda bi, i: (bi, 0, i)),
                pl.BlockSpec((None, kvw, SW_WINDOW), lambda bi, i: (bi, 0, nxt(i)))]
    sink_spec = pl.BlockSpec(memory_space=pltpu.SMEM)
    return pl.pallas_call(
        _sw_kernel,
        grid=(b, s // tq),
        in_specs=[sink_spec, q_spec] + k_specs + vt_specs,
        out_specs=q_spec,
        out_shape=jax.ShapeDtypeStruct((b, s, w), BF16),
        compiler_params=pltpu.CompilerParams(dimension_semantics=("arbitrary", "arbitrary")),
        name="sliding_window_attn",
    )(sink, q, k, k, k, vt, vt, vt)


def _diff_kernel(q_ref, k_ref, vt_ref, lq1_ref, lk1_ref, lq2_ref, lk2_ref, g_ref, o_ref,
                 m_ref, *scratch, lam_init):
    tq = DIFF_Q_TILE
    n_sub = q_ref.shape[0] // tq
    per = len(scratch) // n_sub
    subs = [scratch[i * per:(i + 1) * per] for i in range(n_sub)]
    kv_tile = subs[0][2].shape[0]
    n_kv = k_ref.shape[0] // kv_tile
    cols = 2 * tq
    low = _low_half_mask()

    lam = (jnp.exp(jnp.sum(lq1_ref[...] * lk1_ref[...]))
           - jnp.exp(jnp.sum(lq2_ref[...] * lk2_ref[...])) + lam_init)

    def finalize(sub):
        l_ref, acc_ref = subs[sub][:2]
        o = acc_ref[...] / l_ref[...]
        o = o[:, :tq] - lam * o[:, tq:]
        o = o * lax.rsqrt(jnp.mean(o * o, axis=0, keepdims=True) + EPS) * g_ref[...]
        o_ref[sub * tq:(sub + 1) * tq, :] = (o * (1.0 - lam_init)).T.astype(o_ref.dtype)

    def fast_sweep(sub):
        l_ref, acc_ref, p0_ref, p1_ref, qz_ref = subs[sub]
        p_bufs = (p0_ref, p1_ref)
        q = q_ref[sub * tq:(sub + 1) * tq, :]
        zero = jnp.zeros_like(q)
        qz_ref[0:tq, :] = jnp.where(low, q, zero)
        qz_ref[tq:, :] = jnp.where(low, zero, q)
        ref = None
        unsafe = jnp.zeros((1, cols), F32)
        l8 = None
        beta = None
        pending = None

        def value_product(tile, scale):
            pv = _dot(vt_ref[:, tile * kv_tile:(tile + 1) * kv_tile], p_bufs[tile % 2][...])
            acc_ref[...] = pv if scale is None else scale * acc_ref[...] + pv

        for t in range(n_kv):
            tile_max8 = None
            tile_sum8 = None
            s = _dot_nt(k_ref[t * kv_tile:(t + 1) * kv_tile, :], qz_ref[...])
            if ref is None:
                ref = _col_max(_fold8(s[0:DIFF_WARM_ROWS], jnp.max))
            if pending is not None:
                value_product(*pending)
            for r in range(0, kv_tile, DIFF_PIECE_ROWS):
                piece = s[r:r + DIFF_PIECE_ROWS]
                mx = _fold8(piece, jnp.max)
                tile_max8 = mx if tile_max8 is None else jnp.maximum(tile_max8, mx)
                p = jnp.exp2(piece - ref)
                ps = _fold8(p, jnp.sum)
                tile_sum8 = ps if tile_sum8 is None else tile_sum8 + ps
                p_bufs[t % 2][r:r + DIFF_PIECE_ROWS, :] = p.astype(BF16)
            pending = (t, beta)
            l8 = tile_sum8 if beta is None else beta * l8 + tile_sum8
            tile_max = _col_max(tile_max8)
            unsafe = jnp.maximum(unsafe, jnp.where(tile_max - ref > DIFF_REF_SLACK, 1.0, 0.0))
            new_ref = jnp.maximum(ref, tile_max)
            beta = jnp.exp2(ref - new_ref)
            ref = new_ref
        value_product(*pending)
        l_fast = jnp.sum(l8, axis=0, keepdims=True)
        l_ref[...] = l_fast
        finite = jnp.isfinite(jnp.sum(acc_ref[...], axis=0, keepdims=True) + l_fast)
        return jnp.maximum(unsafe, jnp.where(finite, 0.0, 1.0))

    def exact_sweep(sub):
        l_ref, acc_ref, _, _, qz_ref = subs[sub]
        m_ref[...] = jnp.full(m_ref.shape, -jnp.inf, F32)
        l_ref[...] = jnp.zeros(l_ref.shape, F32)
        acc_ref[...] = jnp.zeros(acc_ref.shape, F32)

        def tile_step(j, carry):
            rows = pl.ds(pl.multiple_of(j * kv_tile, kv_tile), kv_tile)
            s = _dot_nt(k_ref[rows, :], qz_ref[...])
            m_old = m_ref[...]
            m_new = jnp.maximum(m_old, jnp.max(s, axis=0, keepdims=True))
            alpha = jnp.exp2(m_old - m_new)
            p = jnp.exp2(s - m_new)
            m_ref[...] = m_new
            l_ref[...] = alpha * l_ref[...] + jnp.sum(p, axis=0, keepdims=True)
            acc_ref[...] = alpha * acc_ref[...] + _dot(vt_ref[:, rows], p.astype(BF16))
            return carry

        lax.fori_loop(0, n_kv, tile_step, 0)

    unsafe = None
    for sub in range(n_sub):
        flag = fast_sweep(sub)
        unsafe = flag if unsafe is None else jnp.maximum(unsafe, flag)
        finalize(sub)

    @pl.when(jnp.max(unsafe) > 0.0)
    def _():
        for sub in range(n_sub):
            exact_sweep(sub)
            finalize(sub)


def _diff(q, k, vt, lq1, lk1, lq2, lk2, subln_g, lam_init):
    b, s, w = q.shape
    tq = DIFF_Q_TILE
    dv = 2 * HEAD_DIM
    heads = w // LANES
    kv_tile = min(DIFF_KV_TILE, s)
    n_sub = min(DIFF_Q_SUBTILES, s // tq)
    q_spec = pl.BlockSpec((None, n_sub * tq, LANES), lambda bi, h, i: (bi, i, h))
    k_spec = pl.BlockSpec((None, s, LANES), lambda bi, h, i: (bi, 0, h))
    vt_spec = pl.BlockSpec((None, dv, s), lambda bi, h, i: (bi, h, 0))
    vec = lambda n: pl.BlockSpec((1, n), lambda bi, h, i: (0, 0))
    sub_scratch = [pltpu.VMEM((1, 2 * tq), F32), pltpu.VMEM((dv, 2 * tq), F32),
                   pltpu.VMEM((kv_tile, 2 * tq), BF16), pltpu.VMEM((kv_tile, 2 * tq), BF16),
                   pltpu.VMEM((2 * tq, LANES), BF16)]
    return pl.pallas_call(
        functools.partial(_diff_kernel, lam_init=lam_init),
        grid=(b, heads, s // (n_sub * tq)),
        in_specs=[q_spec, k_spec, vt_spec, vec(HEAD_DIM), vec(HEAD_DIM), vec(HEAD_DIM),
                  vec(HEAD_DIM), pl.BlockSpec((dv, 1), lambda bi, h, i: (0, 0))],
        out_specs=q_spec,
        out_shape=jax.ShapeDtypeStruct((b, s, w), BF16),
        scratch_shapes=[pltpu.VMEM((1, 2 * tq), F32)] + sub_scratch * n_sub,
        compiler_params=pltpu.CompilerParams(
            dimension_semantics=("arbitrary", "arbitrary", "arbitrary"),
            vmem_limit_bytes=32 * MIB),
        name="differential_attn",
    )(q, k, vt, lq1, lk1, lq2, lk2, subln_g)


def _merge_kernel(h_ref, u_ref, oa_ref, ob_ref, oc_ref, wg_ref, bg_ref, wb_ref, wo_ref, g_ref,
                  o_ref):
    d = h_ref.shape[1]
    for r0 in range(0, h_ref.shape[0], SUB_ROWS):
        rows = slice(r0, r0 + SUB_ROWS)
        u = u_ref[rows, :]
        merged = None
        for n, br_ref in enumerate((oa_ref, ob_ref, oc_ref)):
            gate = jax.nn.sigmoid(_dot(u, wg_ref[:, n * d:(n + 1) * d]) + bg_ref[n:n + 1, :])
            term = gate * _dot(br_ref[rows, :], wb_ref[n])
            merged = term if merged is None else merged + term
        y = _dot(merged.astype(BF16), wo_ref[...])
        o_ref[rows, :] = h_ref[rows, :] + _rmsnorm(y, g_ref[...])


def _merge(h, u, oa, ob, oc, w_gate, b_gate, w_branch, w_out, post_g):
    t, d = h.shape
    rows = lambda w: pl.BlockSpec((TOKEN_TILE, w), lambda i: (i, 0))
    return pl.pallas_call(
        _merge_kernel,
        grid=(t // TOKEN_TILE,),
        in_specs=[rows(d), rows(d), rows(oa.shape[1]), rows(ob.shape[1]), rows(oc.shape[1]),
                  _resident(w_gate.shape), _resident(b_gate.shape), _resident(w_branch.shape),
                  _resident(w_out.shape), _resident((1, d))],
        out_specs=rows(d),
        out_shape=jax.ShapeDtypeStruct((t, d), F32),
        compiler_params=pltpu.CompilerParams(
            dimension_semantics=("arbitrary",), vmem_limit_bytes=48 * MIB),
        name="gated_merge",
    )(h, u, oa, ob, oc, w_gate, b_gate, w_branch, w_out, post_g)


def _rope_tables(s):
    pos = jnp.arange(s, dtype=F32)
    inv = ROPE_THETA ** (-jnp.arange(0, HEAD_DIM, 2, dtype=F32) / HEAD_DIM)
    ang = pos[:, None] * inv[None, :]
    cos = jnp.tile(jnp.cos(ang), (1, 2 * LANES // HEAD_DIM))
    sin = jnp.tile(jnp.sin(ang), (1, 2 * LANES // HEAD_DIM))
    first_half = (jnp.arange(LANES) % HEAD_DIM) < HEAD_DIM // 2
    return cos, jnp.where(first_half, -sin, 0.0), jnp.where(first_half, 0.0, sin)


def kernel(x, ffn1_pre_g, ffn1_w_gu, ffn1_w_down, ffn1_post_g, mix_pre_g, w_in, na_rpb, sw_sink, diff_lambda_q1, diff_lambda_k1, diff_lambda_q2, diff_lambda_k2, diff_subln_g, w_branch, b_gate, w_out, mix_post_g, ffn2_pre_g, ffn2_w_gu, ffn2_w_down, ffn2_post_g):
    b, s, d = x.shape
    depth = w_in.shape[0]
    assert s % (NA_ROWS_PER_STEP * GRID_W) == 0 and s % TOKEN_TILE == 0 and s >= NA_ROW_K * GRID_W
    assert (b * s) % FFN_TILE == 0
    cos, sin_lo, sin_hi = _rope_tables(s)
    row = lambda a: a.reshape(1, -1).astype(F32)
    h = x.reshape(b * s, d)
    for l in range(depth):
        h = _ffn(h, row(ffn1_pre_g[l]), ffn1_w_gu[l].astype(BF16), ffn1_w_down[l].astype(BF16),
                 row(ffn1_post_g[l]))

        w_l = w_in[l]
        qa_w, ka_w, va_w, qb_w, kb_w, vb_w, qc_w, kc_w, vc_w, gate_w = (
            w_l[:, c0:c1] for c0, c1 in zip(_SPLIT_EDGES[:-1], _SPLIT_EDGES[1:]))
        w_qk = jnp.concatenate([qa_w, ka_w, qb_w, kb_w, qc_w, kc_w], axis=1).astype(BF16)
        w_vt = jnp.concatenate([va_w, vb_w, vc_w], axis=1).T.astype(BF16)
        u, qa, ka, qb, kb, qc, kc, vat, vbt, vct = _proj(
            h, row(mix_pre_g[l]), w_qk, w_vt, cos, sin_lo, sin_hi)
        seq = lambda a: a.reshape(b, s, a.shape[-1])
        oa = _na(seq(qa), seq(ka), vat, _na_bias_table(na_rpb[l], s // GRID_W))
        ob = _sw(seq(qb), seq(kb), vbt, sw_sink[l].reshape(1, -1).astype(F32) * LOG2E)
        lam_init = 0.8 - 0.6 * math.exp(-0.3 * l)
        oc = _diff(seq(qc), seq(kc), vct, row(diff_lambda_q1[l]), row(diff_lambda_k1[l]),
                   row(diff_lambda_q2[l]), row(diff_lambda_k2[l]),
                   diff_subln_g[l].reshape(-1, 1).astype(F32), lam_init)
        flat = lambda a: a.reshape(b * s, a.shape[-1])
        h = _merge(h, u, flat(oa), flat(ob), flat(oc), gate_w.astype(BF16), b_gate[l].astype(F32),
                   w_branch[l].astype(BF16), w_out[l].astype(BF16), row(mix_post_g[l]))

        h = _ffn(h, row(ffn2_pre_g[l]), ffn2_w_gu[l].astype(BF16), ffn2_w_down[l].astype(BF16),
                 row(ffn2_post_g[l]))
    return h.reshape(b, s, d)
```

```python
import functools
import math

import jax
import jax.numpy as jnp
import numpy as np
from jax import lax
from jax.experimental import pallas as pl
from jax.experimental.pallas import tpu as pltpu

F32 = jnp.float32
BF16 = jnp.bfloat16

HEAD_DIM = 64
GRID_W = 64
NA_HEADS = 8
NA_ROW_K = 8
NA_COL_K = 16
SW_Q_HEADS = 8
SW_KV_HEADS = 2
SW_WINDOW = 128
DIFF_HEADS = 4
N_BRANCH = 3
ROPE_THETA = 10000.0
EPS = 1e-6
LOG2E = math.log2(math.e)
NEG = -1e30
LANES = 128
SUBLANES = 8
MIB = 1024 * 1024

A_W = NA_HEADS * HEAD_DIM
B_QW = SW_Q_HEADS * HEAD_DIM
B_KVW = SW_KV_HEADS * HEAD_DIM
C_W = 2 * DIFF_HEADS * HEAD_DIM
_SPLIT_EDGES = (0, A_W, 2 * A_W, 3 * A_W, 3 * A_W + B_QW, 3 * A_W + B_QW + B_KVW,
                3 * A_W + B_QW + 2 * B_KVW, 3 * A_W + B_QW + 2 * B_KVW + C_W,
                3 * A_W + B_QW + 2 * B_KVW + 2 * C_W, 3 * A_W + B_QW + 2 * B_KVW + 3 * C_W, None)

TOKEN_TILE = 1024
FFN_TILE = 1024
SUB_ROWS = 256
NA_ROWS_PER_STEP = 4
SW_Q_TILE = 256
DIFF_Q_TILE = 256
DIFF_KV_TILE = 512
DIFF_Q_SUBTILES = 4
DIFF_PIECE_ROWS = 16
DIFF_WARM_ROWS = 128
DIFF_SUM_LIMIT = 2.0 ** 40
HEAD_PIPELINE_DEPTH = 4


def _rmsnorm(x, g):
    return x * lax.rsqrt(jnp.mean(x * x, axis=-1, keepdims=True) + EPS) * g


def _dot(a, b):
    return jnp.dot(a, b, preferred_element_type=F32)


def _dot_nt(a, b):
    return lax.dot_general(a, b, (((1,), (1,)), ((), ())), preferred_element_type=F32)


def _resident(shape):
    return pl.BlockSpec(shape, lambda *_: (0,) * len(shape), pipeline_mode=pl.Buffered(1))


def _low_half_mask():
    return lax.broadcasted_iota(jnp.int32, (1, LANES), 1) < HEAD_DIM


def _fold8(x, op):
    return op(x.reshape(x.shape[0] // SUBLANES, SUBLANES, x.shape[1]), axis=0)


def _col_max(x8):
    return jnp.max(x8, axis=0, keepdims=True)


def _head_pipeline(heads, scores, attend, depth=HEAD_PIPELINE_DEPTH):
    heads = list(heads)
    ready = [scores(h) for h in heads[:depth]]
    out = []
    for idx, head in enumerate(heads):
        if idx + depth < len(heads):
            ready.append(scores(heads[idx + depth]))
        out.append(attend(head, ready.pop(0)))
    return out


def _ffn_kernel(x_ref, pre_g_ref, wgu_ref, wdown_ref, post_g_ref, o_ref, *, f_chunks, d_ff):
    for r0 in range(0, x_ref.shape[0], SUB_ROWS):
        rows = slice(r0, r0 + SUB_ROWS)
        x = x_ref[rows, :]
        xn = _rmsnorm(x, pre_g_ref[...]).astype(BF16)
        acc = None
        for c0, c1 in f_chunks:
            gate = _dot(xn, wgu_ref[:, c0:c1])
            up = _dot(xn, wgu_ref[:, d_ff + c0:d_ff + c1])
            act = (gate * jax.nn.sigmoid(gate) * up).astype(BF16)
            part = _dot(act, wdown_ref[c0:c1, :])
            acc = part if acc is None else acc + part
        o_ref[rows, :] = x + 0.5 * _rmsnorm(acc, post_g_ref[...])


def _ffn(h, pre_g, w_gu, w_down, post_g):
    t, d = h.shape
    d_ff = w_down.shape[0]
    step = 1024
    f_chunks = tuple((c, min(c + step, d_ff)) for c in range(0, d_ff, step))
    row = pl.BlockSpec((FFN_TILE, d), lambda i: (i, 0))
    return pl.pallas_call(
        functools.partial(_ffn_kernel, f_chunks=f_chunks, d_ff=d_ff),
        grid=(t // FFN_TILE,),
        in_specs=[row, _resident((1, d)), _resident(w_gu.shape), _resident(w_down.shape),
                  _resident((1, d))],
        out_specs=row,
        out_shape=jax.ShapeDtypeStruct((t, d), F32),
        compiler_params=pltpu.CompilerParams(
            dimension_semantics=("arbitrary",), vmem_limit_bytes=48 * MIB),
        name="ffn",
    )(h, pre_g, w_gu, w_down, post_g)


def _rope(t, cos, sin_lo, sin_hi):
    out = []
    for c in range(t.shape[1] // LANES):
        tc = t[:, c * LANES:(c + 1) * LANES]
        fwd = pltpu.roll(tc, LANES - HEAD_DIM // 2, axis=1)
        back = pltpu.roll(tc, HEAD_DIM // 2, axis=1)
        out.append(tc * cos + fwd * sin_lo + back * sin_hi)
    return jnp.concatenate(out, axis=1)


def _proj_kernel(h_ref, g_ref, wqk_ref, wvt_ref, cos_ref, sin_lo_ref, sin_hi_ref,
                 u_ref, qa_ref, ka_ref, qb_ref, kb_ref, qc_ref, kc_ref, vat_ref, vbt_ref, vct_ref):
    scale = HEAD_DIM ** -0.5 * LOG2E
    for r0 in range(0, h_ref.shape[0], SUB_ROWS):
        rows = slice(r0, r0 + SUB_ROWS)
        un = _rmsnorm(h_ref[rows, :], g_ref[...]).astype(BF16)
        u_ref[rows, :] = un
        cos, sin_lo, sin_hi = cos_ref[rows, :], sin_lo_ref[rows, :], sin_hi_ref[rows, :]
        col = 0
        for out_ref, rotary, scaled in ((qa_ref, False, True), (ka_ref, False, False),
                                        (qb_ref, True, True), (kb_ref, True, False),
                                        (qc_ref, True, True), (kc_ref, True, False)):
            width = out_ref.shape[1]
            t = _dot(un, wqk_ref[:, col:col + width])
            col += width
            if rotary:
                t = _rope(t, cos, sin_lo, sin_hi)
            out_ref[rows, :] = (t * scale if scaled else t).astype(BF16)
        row = 0
        for vt_ref in (vat_ref, vbt_ref, vct_ref):
            width = vt_ref.shape[0]
            vt_ref[:, rows] = _dot_nt(wvt_ref[row:row + width, :], un).astype(BF16)
            row += width


def _proj(h, g, w_qk, w_vt, cos, sin_lo, sin_hi):
    t, d = h.shape
    s = cos.shape[0]
    tiles_per_seq = s // TOKEN_TILE
    widths = (d, A_W, A_W, B_QW, B_KVW, C_W, C_W)
    v_widths = (A_W, B_KVW, C_W)
    rows = lambda w: pl.BlockSpec((TOKEN_TILE, w), lambda i: (i, 0))
    chans = lambda w: pl.BlockSpec((None, w, TOKEN_TILE),
                                   lambda i: (i // tiles_per_seq, 0, i % tiles_per_seq))
    table = pl.BlockSpec((TOKEN_TILE, LANES), lambda i: (i % tiles_per_seq, 0))
    return pl.pallas_call(
        _proj_kernel,
        grid=(t // TOKEN_TILE,),
        in_specs=[rows(d), _resident((1, d)), _resident(w_qk.shape), _resident(w_vt.shape),
                  table, table, table],
        out_specs=[rows(w) for w in widths] + [chans(w) for w in v_widths],
        out_shape=([jax.ShapeDtypeStruct((t, w), BF16) for w in widths]
                   + [jax.ShapeDtypeStruct((t // s, w, s), BF16) for w in v_widths]),
        compiler_params=pltpu.CompilerParams(
            dimension_semantics=("arbitrary",), vmem_limit_bytes=48 * MIB),
        name="proj",
    )(h, g, w_qk, w_vt, cos, sin_lo, sin_hi)


def _na_kernel(q_ref, kp_ref, kc_ref, kn_ref, vtp_ref, vtc_ref, vtn_ref, bias_ref, o_ref):
    blk = q_ref.shape[0]
    k_refs = (kp_ref, kc_ref, kn_ref)
    vt_refs = (vtp_ref, vtc_ref, vtn_ref)
    low = _low_half_mask()
    top = lax.broadcasted_iota(jnp.int32, (LANES, 1), 0) < HEAD_DIM
    lanes = lambda head: slice(head // 2 * LANES, (head // 2 + 1) * LANES)

    def scores(head):
        qc = q_ref[:, lanes(head)]
        qz = jnp.where(low if head % 2 == 0 else ~low, qc, jnp.zeros_like(qc))
        return [_dot_nt(k_refs[w][:, lanes(head)], qz) + bias_ref[head, w * blk:(w + 1) * blk, :]
                for w in range(3)]

    def attend(head, s):
        m = jnp.max(jnp.maximum(jnp.maximum(s[0], s[1]), s[2]), axis=0, keepdims=True)
        l8, o = None, None
        for w in range(3):
            p = jnp.exp2(s[w] - m)
            ps = _fold8(p, jnp.sum)
            l8 = ps if l8 is None else l8 + ps
            pv = _dot(vt_refs[w][lanes(head), :], p.astype(BF16))
            o = pv if o is None else o + pv
        return o / jnp.sum(l8, axis=0, keepdims=True)

    res = _head_pipeline(range(NA_HEADS), scores, attend)
    for c in range(A_W // LANES):
        o_ref[:, c * LANES:(c + 1) * LANES] = (
            jnp.where(top, res[2 * c], res[2 * c + 1]).T.astype(o_ref.dtype))


def _na_bias_table(rpb, grid_rows):
    n = NA_ROWS_PER_STEP
    c = np.arange(GRID_W)
    dc = np.clip(c[:, None] - c[None, :] + (NA_COL_K - 1), 0, 2 * NA_COL_K - 2)
    col_start = np.clip(c - NA_COL_K // 2, 0, GRID_W - NA_COL_K)
    col_ok = (c[:, None] >= col_start[None, :]) & (c[:, None] < col_start[None, :] + NA_COL_K)
    one_hot = jnp.asarray(dc[None] == np.arange(2 * NA_COL_K - 1)[:, None, None], F32)
    expanded = jnp.einsum("hdj,jkq->hdkq", rpb.astype(F32), one_hot,
                          precision=lax.Precision.HIGHEST)
    by_offset = jnp.where(col_ok, expanded * LOG2E, NEG)
    masked = jnp.full_like(by_offset[:, 0], NEG)
    tables = []
    for first_row in (0, n, grid_rows - n):
        key_rows = []
        for kr in range(3 * n):
            blocks = []
            for qr in range(n):
                q_abs, k_abs = first_row + qr, first_row - n + kr
                row_start = min(max(q_abs - NA_ROW_K // 2, 0), grid_rows - NA_ROW_K)
                in_window = row_start <= k_abs < row_start + NA_ROW_K
                blocks.append(by_offset[:, k_abs - q_abs + NA_ROW_K - 1] if in_window else masked)
            key_rows.append(jnp.concatenate(blocks, axis=-1))
        tables.append(jnp.concatenate(key_rows, axis=1))
    return jnp.stack(tables)


def _na(q, k, vt, bias):
    b, s, w = q.shape
    blk = NA_ROWS_PER_STEP * GRID_W
    n_blk = s // blk
    clamp = lambda j: jnp.clip(j, 0, n_blk - 1)
    tok = lambda shift: pl.BlockSpec((None, blk, w), lambda bi, j: (bi, clamp(j + shift), 0))
    chan = lambda shift: pl.BlockSpec((None, w, blk), lambda bi, j: (bi, 0, clamp(j + shift)))
    table = pl.BlockSpec(
        (None,) + bias.shape[1:],
        lambda bi, j: (jnp.where(j == 0, 0, jnp.where(j == n_blk - 1, 2, 1)), 0, 0, 0))
    return pl.pallas_call(
        _na_kernel,
        grid=(b, n_blk),
        in_specs=[tok(0), tok(-1), tok(0), tok(1), chan(-1), chan(0), chan(1), table],
        out_specs=tok(0),
        out_shape=jax.ShapeDtypeStruct((b, s, w), BF16),
        compiler_params=pltpu.CompilerParams(
            dimension_semantics=("arbitrary", "arbitrary"), vmem_limit_bytes=40 * MIB),
        name="neighbourhood_attn",
    )(q, k, k, k, vt, vt, vt, bias)


def _sw_kernel(sink_ref, q_ref, kp_ref, kc_ref, kn_ref, vp_ref, vc_ref, vn_ref, o_ref):
    i = pl.program_id(1)
    tq = q_ref.shape[0]
    win = SW_WINDOW
    k = jnp.concatenate([kp_ref[...], kc_ref[...], kn_ref[...]], axis=0)
    vt = jnp.concatenate([vp_ref[...], vc_ref[...], vn_ref[...]], axis=1)
    n_kblk = k.shape[0] // win
    key_row = lax.broadcasted_iota(jnp.int32, (win, win), 0)
    query_col = lax.broadcasted_iota(jnp.int32, (win, win), 1)
    band = {0: jnp.where(key_row >= query_col, 0.0, NEG), 1: None,
            2: jnp.where(key_row <= query_col, 0.0, NEG)}
    outside = {0: jnp.where(i == 0, NEG, 0.0),
               n_kblk - 1: jnp.where(i == pl.num_programs(1) - 1, NEG, 0.0)}
    masks = {}
    for c in range(tq // win):
        for d in range(3):
            a = c + d
            mask = band[d]
            if a in outside:
                mask = outside[a] if mask is None else mask + outside[a]
            masks[a, c] = mask
    low = _low_half_mask()
    group = SW_Q_HEADS // SW_KV_HEADS

    def scores(head):
        kv_head = head // group
        qc = q_ref[:, head // 2 * LANES:(head // 2 + 1) * LANES]
        src = qc if head % 2 == kv_head else pltpu.roll(qc, HEAD_DIM, axis=1)
        qz = jnp.where(low if kv_head == 0 else ~low, src, jnp.zeros_like(src))
        return _dot_nt(k, qz)

    def attend(head, s):
        kv_head = head // group
        sink = sink_ref[0, head]
        p_cols, l_cols = [], []
        for c in range(tq // win):
            blocks = []
            for a in range(c, c + 3):
                blk = s[a * win:(a + 1) * win, c * win:(c + 1) * win]
                blocks.append(blk if masks[a, c] is None else blk + masks[a, c])
            m = jnp.maximum(jnp.max(jnp.maximum(jnp.maximum(blocks[0], blocks[1]), blocks[2]),
                                    axis=0, keepdims=True), sink)
            l8, packed = None, []
            for blk in blocks:
                p = jnp.exp2(blk - m)
                ps = _fold8(p, jnp.sum)
                l8 = ps if l8 is None else l8 + ps
                packed.append(p.astype(BF16))
            l_cols.append(jnp.sum(l8, axis=0, keepdims=True) + jnp.exp2(sink - m))
            empty = jnp.zeros((win, win), BF16)
            p_cols.append(jnp.concatenate(
                [empty] * c + packed + [empty] * (n_kblk - c - 3), axis=0))
        p_all = jnp.concatenate(p_cols, axis=1)
        o = _dot(vt, p_all) / jnp.concatenate(l_cols, axis=1)
        return o[kv_head * HEAD_DIM:(kv_head + 1) * HEAD_DIM]

    res = _head_pipeline(range(SW_Q_HEADS), scores, attend)
    for c in range(B_QW // LANES):
        o_ref[:, c * LANES:(c + 1) * LANES] = (
            jnp.concatenate(res[2 * c:2 * c + 2], axis=0).T.astype(o_ref.dtype))


def _sw(q, k, vt, sink):
    b, s, w = q.shape
    kvw = k.shape[-1]
    tq = SW_Q_TILE
    per = tq // SW_WINDOW
    n_blk = s // SW_WINDOW
    prev = lambda i: jnp.maximum(per * i - 1, 0)
    nxt = lambda i: jnp.minimum(per * (i + 1), n_blk - 1)
    q_spec = pl.BlockSpec((None, tq, w), lambda bi, i: (bi, i, 0))
    k_specs = [pl.BlockSpec((None, SW_WINDOW, kvw), lambda bi, i: (bi, prev(i), 0)),
               pl.BlockSpec((None, tq, kvw), lambda bi, i: (bi, i, 0)),
               pl.BlockSpec((None, SW_WINDOW, kvw), lambda bi, i: (bi, nxt(i), 0))]
    vt_specs = [pl.BlockSpec((None, kvw, SW_WINDOW), lambda bi, i: (bi, 0, prev(i))),
                pl.BlockSpec((None, kvw, tq), lambda bi, i: (bi, 0, i)),
                pl.BlockSpec((None, kvw, SW_WINDOW), lambda bi, i: (bi, 0, nxt(i)))]
    sink_spec = pl.BlockSpec(memory_space=pltpu.SMEM)
    return pl.pallas_call(
        _sw_kernel,
        grid=(b, s // tq),
        in_specs=[sink_spec, q_spec] + k_specs + vt_specs,
        out_specs=q_spec,
        out_shape=jax.ShapeDtypeStruct((b, s, w), BF16),
        compiler_params=pltpu.CompilerParams(dimension_semantics=("arbitrary", "arbitrary")),
        name="sliding_window_attn",
    )(sink, q, k, k, k, vt, vt, vt)


def _diff_kernel(q_ref, k_ref, vt_ref, lq1_ref, lk1_ref, lq2_ref, lk2_ref, g_ref, o_ref,
                 m_ref, *scratch, lam_init):
    tq = DIFF_Q_TILE
    n_sub = q_ref.shape[0] // tq
    per = len(scratch) // n_sub
    subs = [scratch[i * per:(i + 1) * per] for i in range(n_sub)]
    kv_tile = subs[0][2].shape[0]
    n_kv = k_ref.shape[0] // kv_tile
    cols = 2 * tq
    low = _low_half_mask()

    lam = (jnp.exp(jnp.sum(lq1_ref[...] * lk1_ref[...]))
           - jnp.exp(jnp.sum(lq2_ref[...] * lk2_ref[...])) + lam_init)

    def finalize(sub):
        l_ref, acc_ref = subs[sub][:2]
        o = acc_ref[...] / l_ref[...]
        o = o[:, :tq] - lam * o[:, tq:]
        o = o * lax.rsqrt(jnp.mean(o * o, axis=0, keepdims=True) + EPS) * g_ref[...]
        o_ref[sub * tq:(sub + 1) * tq, :] = (o * (1.0 - lam_init)).T.astype(o_ref.dtype)

    def fast_sweep(sub):
        l_ref, acc_ref, p0_ref, p1_ref, qz_ref = subs[sub]
        p_bufs = (p0_ref, p1_ref)
        q = q_ref[sub * tq:(sub + 1) * tq, :]
        zero = jnp.zeros_like(q)
        qz_ref[0:tq, :] = jnp.where(low, q, zero)
        qz_ref[tq:, :] = jnp.where(low, zero, q)
        ref = None
        unsafe = jnp.zeros((1, cols), F32)
        l8 = None
        beta = None
        pending = None

        def value_product(tile, scale):
            pv = _dot(vt_ref[:, tile * kv_tile:(tile + 1) * kv_tile], p_bufs[tile % 2][...])
            acc_ref[...] = pv if scale is None else scale * acc_ref[...] + pv

        for t in range(n_kv):
            tile_sum8 = None
            s = _dot_nt(k_ref[t * kv_tile:(t + 1) * kv_tile, :], qz_ref[...])
            if ref is None:
                ref = _col_max(_fold8(s[0:DIFF_WARM_ROWS], jnp.max))
            if pending is not None:
                value_product(*pending)
            for r in range(0, kv_tile, DIFF_PIECE_ROWS):
                p = jnp.exp2(s[r:r + DIFF_PIECE_ROWS] - ref)
                ps = _fold8(p, jnp.sum)
                tile_sum8 = ps if tile_sum8 is None else tile_sum8 + ps
                p_bufs[t % 2][r:r + DIFF_PIECE_ROWS, :] = p.astype(BF16)
            pending = (t, beta)
            l8 = tile_sum8 if beta is None else beta * l8 + tile_sum8
            grow = jnp.maximum(jnp.sum(tile_sum8, axis=0, keepdims=True), 1.0)
            unsafe = jnp.maximum(unsafe, jnp.where(grow > DIFF_SUM_LIMIT, 1.0, 0.0))
            beta = 1.0 / grow
            ref = ref + jnp.log2(grow)
        value_product(*pending)
        l_fast = jnp.sum(l8, axis=0, keepdims=True)
        l_ref[...] = l_fast
        finite = jnp.isfinite(jnp.sum(acc_ref[...], axis=0, keepdims=True) + l_fast)
        return jnp.maximum(unsafe, jnp.where(finite, 0.0, 1.0))

    def exact_sweep(sub):
        l_ref, acc_ref, _, _, qz_ref = subs[sub]
        m_ref[...] = jnp.full(m_ref.shape, -jnp.inf, F32)
        l_ref[...] = jnp.zeros(l_ref.shape, F32)
        acc_ref[...] = jnp.zeros(acc_ref.shape, F32)

        def tile_step(j, carry):
            rows = pl.ds(pl.multiple_of(j * kv_tile, kv_tile), kv_tile)
            s = _dot_nt(k_ref[rows, :], qz_ref[...])
            m_old = m_ref[...]
            m_new = jnp.maximum(m_old, jnp.max(s, axis=0, keepdims=True))
            alpha = jnp.exp2(m_old - m_new)
            p = jnp.exp2(s - m_new)
            m_ref[...] = m_new
            l_ref[...] = alpha * l_ref[...] + jnp.sum(p, axis=0, keepdims=True)
            acc_ref[...] = alpha * acc_ref[...] + _dot(vt_ref[:, rows], p.astype(BF16))
            return carry

        lax.fori_loop(0, n_kv, tile_step, 0)

    unsafe = None
    for sub in range(n_sub):
        flag = fast_sweep(sub)
        unsafe = flag if unsafe is None else jnp.maximum(unsafe, flag)
        finalize(sub)

    @pl.when(jnp.max(unsafe) > 0.0)
    def _():
        for sub in range(n_sub):
            exact_sweep(sub)
            finalize(sub)


def _diff(q, k, vt, lq1, lk1, lq2, lk2, subln_g, lam_init):
    b, s, w = q.shape
    tq = DIFF_Q_TILE
    dv = 2 * HEAD_DIM
    heads = w // LANES
    kv_tile = min(DIFF_KV_TILE, s)
    n_sub = min(DIFF_Q_SUBTILES, s // tq)
    q_spec = pl.BlockSpec((None, n_sub * tq, LANES), lambda bi, h, i: (bi, i, h))
    k_spec = pl.BlockSpec((None, s, LANES), lambda bi, h, i: (bi, 0, h))
    vt_spec = pl.BlockSpec((None, dv, s), lambda bi, h, i: (bi, h, 0))
    vec = lambda n: pl.BlockSpec((1, n), lambda bi, h, i: (0, 0))
    sub_scratch = [pltpu.VMEM((1, 2 * tq), F32), pltpu.VMEM((dv, 2 * tq), F32),
                   pltpu.VMEM((kv_tile, 2 * tq), BF16), pltpu.VMEM((kv_tile, 2 * tq), BF16),
                   pltpu.VMEM((2 * tq, LANES), BF16)]
    return pl.pallas_call(
        functools.partial(_diff_kernel, lam_init=lam_init),
        grid=(b, heads, s // (n_sub * tq)),
        in_specs=[q_spec, k_spec, vt_spec, vec(HEAD_DIM), vec(HEAD_DIM), vec(HEAD_DIM),
                  vec(HEAD_DIM), pl.BlockSpec((dv, 1), lambda bi, h, i: (0, 0))],
        out_specs=q_spec,
        out_shape=jax.ShapeDtypeStruct((b, s, w), BF16),
        scratch_shapes=[pltpu.VMEM((1, 2 * tq), F32)] + sub_scratch * n_sub,
        compiler_params=pltpu.CompilerParams(
            dimension_semantics=("arbitrary", "arbitrary", "arbitrary"),
            vmem_limit_bytes=32 * MIB),
        name="differential_attn",
    )(q, k, vt, lq1, lk1, lq2, lk2, subln_g)


def _merge_kernel(h_ref, u_ref, oa_ref, ob_ref, oc_ref, wg_ref, bg_ref, wb_ref, wo_ref, g_ref,
                  o_ref):
    d = h_ref.shape[1]
    for r0 in range(0, h_ref.shape[0], SUB_ROWS):
        rows = slice(r0, r0 + SUB_ROWS)
        u = u_ref[rows, :]
        merged = None
        for n, br_ref in enumerate((oa_ref, ob_ref, oc_ref)):
            gate = jax.nn.sigmoid(_dot(u, wg_ref[:, n * d:(n + 1) * d]) + bg_ref[n:n + 1, :])
            term = gate * _dot(br_ref[rows, :], wb_ref[n])
            merged = term if merged is None else merged + term
        y = _dot(merged.astype(BF16), wo_ref[...])
        o_ref[rows, :] = h_ref[rows, :] + _rmsnorm(y, g_ref[...])


def _merge(h, u, oa, ob, oc, w_gate, b_gate, w_branch, w_out, post_g):
    t, d = h.shape
    rows = lambda w: pl.BlockSpec((TOKEN_TILE, w), lambda i: (i, 0))
    return pl.pallas_call(
        _merge_kernel,
        grid=(t // TOKEN_TILE,),
        in_specs=[rows(d), rows(d), rows(oa.shape[1]), rows(ob.shape[1]), rows(oc.shape[1]),
                  _resident(w_gate.shape), _resident(b_gate.shape), _resident(w_branch.shape),
                  _resident(w_out.shape), _resident((1, d))],
        out_specs=rows(d),
        out_shape=jax.ShapeDtypeStruct((t, d), F32),
        compiler_params=pltpu.CompilerParams(
            dimension_semantics=("arbitrary",), vmem_limit_bytes=48 * MIB),
        name="gated_merge",
    )(h, u, oa, ob, oc, w_gate, b_gate, w_branch, w_out, post_g)


def _rope_tables(s):
    pos = jnp.arange(s, dtype=F32)
    inv = ROPE_THETA ** (-jnp.arange(0, HEAD_DIM, 2, dtype=F32) / HEAD_DIM)
    ang = pos[:, None] * inv[None, :]
    cos = jnp.tile(jnp.cos(ang), (1, 2 * LANES // HEAD_DIM))
    sin = jnp.tile(jnp.sin(ang), (1, 2 * LANES // HEAD_DIM))
    first_half = (jnp.arange(LANES) % HEAD_DIM) < HEAD_DIM // 2
    return cos, jnp.where(first_half, -sin, 0.0), jnp.where(first_half, 0.0, sin)


def kernel(x, ffn1_pre_g, ffn1_w_gu, ffn1_w_down, ffn1_post_g, mix_pre_g, w_in, na_rpb, sw_sink, diff_lambda_q1, diff_lambda_k1, diff_lambda_q2, diff_lambda_k2, diff_subln_g, w_branch, b_gate, w_out, mix_post_g, ffn2_pre_g, ffn2_w_gu, ffn2_w_down, ffn2_post_g):
    b, s, d = x.shape
    depth = w_in.shape[0]
    assert s % (NA_ROWS_PER_STEP * GRID_W) == 0 and s % TOKEN_TILE == 0 and s >= NA_ROW_K * GRID_W
    assert (b * s) % FFN_TILE == 0
    cos, sin_lo, sin_hi = _rope_tables(s)
    row = lambda a: a.reshape(1, -1).astype(F32)
    h = x.reshape(b * s, d)
    for l in range(depth):
        h = _ffn(h, row(ffn1_pre_g[l]), ffn1_w_gu[l].astype(BF16), ffn1_w_down[l].astype(BF16),
                 row(ffn1_post_g[l]))

        w_l = w_in[l]
        qa_w, ka_w, va_w, qb_w, kb_w, vb_w, qc_w, kc_w, vc_w, gate_w = (
            w_l[:, c0:c1] for c0, c1 in zip(_SPLIT_EDGES[:-1], _SPLIT_EDGES[1:]))
        w_qk = jnp.concatenate([qa_w, ka_w, qb_w, kb_w, qc_w, kc_w], axis=1).astype(BF16)
        w_vt = jnp.concatenate([va_w, vb_w, vc_w], axis=1).T.astype(BF16)
        u, qa, ka, qb, kb, qc, kc, vat, vbt, vct = _proj(
            h, row(mix_pre_g[l]), w_qk, w_vt, cos, sin_lo, sin_hi)
        seq = lambda a: a.reshape(b, s, a.shape[-1])
        oa = _na(seq(qa), seq(ka), vat, _na_bias_table(na_rpb[l], s // GRID_W))
        ob = _sw(seq(qb), seq(kb), vbt, sw_sink[l].reshape(1, -1).astype(F32) * LOG2E)
        lam_init = 0.8 - 0.6 * math.exp(-0.3 * l)
        oc = _diff(seq(qc), seq(kc), vct, row(diff_lambda_q1[l]), row(diff_lambda_k1[l]),
                   row(diff_lambda_q2[l]), row(diff_lambda_k2[l]),
                   diff_subln_g[l].reshape(-1, 1).astype(F32), lam_init)
        flat = lambda a: a.reshape(b * s, a.shape[-1])
        h = _merge(h, u, flat(oa), flat(ob), flat(oc), gate_w.astype(BF16), b_gate[l].astype(F32),
                   w_branch[l].astype(BF16), w_out[l].astype(BF16), row(mix_post_g[l]))

        h = _ffn(h, row(ffn2_pre_g[l]), ffn2_w_gu[l].astype(BF16), ffn2_w_down[l].astype(BF16),
                 row(ffn2_post_g[l]))
    return h.reshape(b, s, d)
```

```python
import functools
import math

import jax
import jax.numpy as jnp
import numpy as np
from jax import lax
from jax.experimental import pallas as pl
from jax.experimental.pallas import tpu as pltpu

F32 = jnp.float32
BF16 = jnp.bfloat16

HEAD_DIM = 64
GRID_W = 64
NA_HEADS = 8
NA_ROW_K = 8
NA_COL_K = 16
SW_Q_HEADS = 8
SW_KV_HEADS = 2
SW_WINDOW = 128
DIFF_HEADS = 4
N_BRANCH = 3
ROPE_THETA = 10000.0
EPS = 1e-6
LOG2E = math.log2(math.e)
NEG = -1e30
LANES = 128
SUBLANES = 8
MIB = 1024 * 1024

A_W = NA_HEADS * HEAD_DIM
B_QW = SW_Q_HEADS * HEAD_DIM
B_KVW = SW_KV_HEADS * HEAD_DIM
C_W = 2 * DIFF_HEADS * HEAD_DIM
_SPLIT_EDGES = (0, A_W, 2 * A_W, 3 * A_W, 3 * A_W + B_QW, 3 * A_W + B_QW + B_KVW,
                3 * A_W + B_QW + 2 * B_KVW, 3 * A_W + B_QW + 2 * B_KVW + C_W,
                3 * A_W + B_QW + 2 * B_KVW + 2 * C_W, 3 * A_W + B_QW + 2 * B_KVW + 3 * C_W, None)

TOKEN_TILE = 1024
FFN_TILE = 1024
SUB_ROWS = 256
NA_ROWS_PER_STEP = 4
SW_Q_TILE = 256
DIFF_Q_TILE = 256
DIFF_KV_TILE = 512
DIFF_Q_SUBTILES = 4
DIFF_PIECE_ROWS = 16
DIFF_WARM_ROWS = 128
DIFF_SUM_LIMIT = 2.0 ** 40
HEAD_PIPELINE_DEPTH = 4


def _rmsnorm(x, g):
    return x * lax.rsqrt(jnp.mean(x * x, axis=-1, keepdims=True) + EPS) * g


def _dot(a, b):
    return jnp.dot(a, b, preferred_element_type=F32)


def _dot_nt(a, b):
    return lax.dot_general(a, b, (((1,), (1,)), ((), ())), preferred_element_type=F32)


def _resident(shape):
    return pl.BlockSpec(shape, lambda *_: (0,) * len(shape), pipeline_mode=pl.Buffered(1))


def _low_half_mask():
    return lax.broadcasted_iota(jnp.int32, (1, LANES), 1) < HEAD_DIM


def _fold8(x, op):
    return op(x.reshape(x.shape[0] // SUBLANES, SUBLANES, x.shape[1]), axis=0)


def _col_max(x8):
    return jnp.max(x8, axis=0, keepdims=True)


def _head_pipeline(heads, scores, attend, depth=HEAD_PIPELINE_DEPTH):
    heads = list(heads)
    ready = [scores(h) for h in heads[:depth]]
    out = []
    for idx, head in enumerate(heads):
        if idx + depth < len(heads):
            ready.append(scores(heads[idx + depth]))
        out.append(attend(head, ready.pop(0)))
    return out


def _ffn_kernel(x_ref, pre_g_ref, wgu_ref, wdown_ref, post_g_ref, o_ref, *, f_chunks, d_ff):
    for r0 in range(0, x_ref.shape[0], SUB_ROWS):
        rows = slice(r0, r0 + SUB_ROWS)
        x = x_ref[rows, :]
        xn = _rmsnorm(x, pre_g_ref[...]).astype(BF16)
        acc = None
        for c0, c1 in f_chunks:
            gate = _dot(xn, wgu_ref[:, c0:c1])
            up = _dot(xn, wgu_ref[:, d_ff + c0:d_ff + c1])
            act = (gate * jax.nn.sigmoid(gate) * up).astype(BF16)
            part = _dot(act, wdown_ref[c0:c1, :])
            acc = part if acc is None else acc + part
        o_ref[rows, :] = x + 0.5 * _rmsnorm(acc, post_g_ref[...])


def _ffn(h, pre_g, w_gu, w_down, post_g):
    t, d = h.shape
    d_ff = w_down.shape[0]
    step = 1024
    f_chunks = tuple((c, min(c + step, d_ff)) for c in range(0, d_ff, step))
    row = pl.BlockSpec((FFN_TILE, d), lambda i: (i, 0))
    return pl.pallas_call(
        functools.partial(_ffn_kernel, f_chunks=f_chunks, d_ff=d_ff),
        grid=(t // FFN_TILE,),
        in_specs=[row, _resident((1, d)), _resident(w_gu.shape), _resident(w_down.shape),
                  _resident((1, d))],
        out_specs=row,
        out_shape=jax.ShapeDtypeStruct((t, d), F32),
        compiler_params=pltpu.CompilerParams(
            dimension_semantics=("arbitrary",), vmem_limit_bytes=48 * MIB),
        name="ffn",
    )(h, pre_g, w_gu, w_down, post_g)


def _rope(t, cos, sin_lo, sin_hi):
    out = []
    for c in range(t.shape[1] // LANES):
        tc = t[:, c * LANES:(c + 1) * LANES]
        fwd = pltpu.roll(tc, LANES - HEAD_DIM // 2, axis=1)
        back = pltpu.roll(tc, HEAD_DIM // 2, axis=1)
        out.append(tc * cos + fwd * sin_lo + back * sin_hi)
    return jnp.concatenate(out, axis=1)


def _proj_kernel(h_ref, g_ref, wqk_ref, wvt_ref, cos_ref, sin_lo_ref, sin_hi_ref,
                 u_ref, qa_ref, ka_ref, qb_ref, kb_ref, qc_ref, kc_ref, vat_ref, vbt_ref, vct_ref):
    scale = HEAD_DIM ** -0.5 * LOG2E
    for r0 in range(0, h_ref.shape[0], SUB_ROWS):
        rows = slice(r0, r0 + SUB_ROWS)
        un = _rmsnorm(h_ref[rows, :], g_ref[...]).astype(BF16)
        u_ref[rows, :] = un
        cos, sin_lo, sin_hi = cos_ref[rows, :], sin_lo_ref[rows, :], sin_hi_ref[rows, :]
        col = 0
        for out_ref, rotary, scaled in ((qa_ref, False, True), (ka_ref, False, False),
                                        (qb_ref, True, True), (kb_ref, True, False),
                                        (qc_ref, True, True), (kc_ref, True, False)):
            width = out_ref.shape[1]
            t = _dot(un, wqk_ref[:, col:col + width])
            col += width
            if rotary:
                t = _rope(t, cos, sin_lo, sin_hi)
            out_ref[rows, :] = (t * scale if scaled else t).astype(BF16)
        row = 0
        for vt_ref in (vat_ref, vbt_ref, vct_ref):
            width = vt_ref.shape[0]
            vt_ref[:, rows] = _dot_nt(wvt_ref[row:row + width, :], un).astype(BF16)
            row += width


def _proj(h, g, w_qk, w_vt, cos, sin_lo, sin_hi):
    t, d = h.shape
    s = cos.shape[0]
    tiles_per_seq = s // TOKEN_TILE
    widths = (d, A_W, A_W, B_QW, B_KVW, C_W, C_W)
    v_widths = (A_W, B_KVW, C_W)
    rows = lambda w: pl.BlockSpec((TOKEN_TILE, w), lambda i: (i, 0))
    chans = lambda w: pl.BlockSpec((None, w, TOKEN_TILE),
                                   lambda i: (i // tiles_per_seq, 0, i % tiles_per_seq))
    table = pl.BlockSpec((TOKEN_TILE, LANES), lambda i: (i % tiles_per_seq, 0))
    return pl.pallas_call(
        _proj_kernel,
        grid=(t // TOKEN_TILE,),
        in_specs=[rows(d), _resident((1, d)), _resident(w_qk.shape), _resident(w_vt.shape),
                  table, table, table],
        out_specs=[rows(w) for w in widths] + [chans(w) for w in v_widths],
        out_shape=([jax.ShapeDtypeStruct((t, w), BF16) for w in widths]
                   + [jax.ShapeDtypeStruct((t // s, w, s), BF16) for w in v_widths]),
        compiler_params=pltpu.CompilerParams(
            dimension_semantics=("arbitrary",), vmem_limit_bytes=48 * MIB),
        name="proj",
    )(h, g, w_qk, w_vt, cos, sin_lo, sin_hi)


def _na_stage(q_ref, kp_ref, kc_ref, kn_ref, vtp_ref, vtc_ref, vtn_ref, bias_ref, o_ref):
    blk = q_ref.shape[0]
    k_refs = (kp_ref, kc_ref, kn_ref)
    vt_refs = (vtp_ref, vtc_ref, vtn_ref)
    low = _low_half_mask()
    top = lax.broadcasted_iota(jnp.int32, (LANES, 1), 0) < HEAD_DIM
    lanes = lambda head: slice(head // 2 * LANES, (head // 2 + 1) * LANES)

    def scores(head):
        qc = q_ref[:, lanes(head)]
        qz = jnp.where(low if head % 2 == 0 else ~low, qc, jnp.zeros_like(qc))
        return [_dot_nt(k_refs[w][:, lanes(head)], qz) + bias_ref[head, w * blk:(w + 1) * blk, :]
                for w in range(3)]

    ones_rows = jnp.ones((2 * SUBLANES, blk), BF16)
    vt_ones = {}

    def values_and_ones(w, chunk):
        if (w, chunk) not in vt_ones:
            vt_ones[w, chunk] = jnp.concatenate(
                [vt_refs[w][chunk * LANES:(chunk + 1) * LANES, :], ones_rows], axis=0)
        return vt_ones[w, chunk]

    def attend(head, s):
        m = jnp.max(jnp.maximum(jnp.maximum(s[0], s[1]), s[2]), axis=0, keepdims=True)
        o = None
        for w in range(3):
            pv = _dot(values_and_ones(w, head // 2), jnp.exp2(s[w] - m).astype(BF16))
            o = pv if o is None else o + pv
        return o[:LANES] / o[LANES:LANES + 1]

    def finish(res):
        for c in range(A_W // LANES):
            o_ref[:, c * LANES:(c + 1) * LANES] = (
                jnp.where(top, res[2 * c], res[2 * c + 1]).T.astype(o_ref.dtype))

    return scores, attend, finish


def _na_bias_table(rpb, grid_rows):
    n = NA_ROWS_PER_STEP
    c = np.arange(GRID_W)
    dc = np.clip(c[:, None] - c[None, :] + (NA_COL_K - 1), 0, 2 * NA_COL_K - 2)
    col_start = np.clip(c - NA_COL_K // 2, 0, GRID_W - NA_COL_K)
    col_ok = (c[:, None] >= col_start[None, :]) & (c[:, None] < col_start[None, :] + NA_COL_K)
    one_hot = jnp.asarray(dc[None] == np.arange(2 * NA_COL_K - 1)[:, None, None], F32)
    expanded = jnp.einsum("hdj,jkq->hdkq", rpb.astype(F32), one_hot,
                          precision=lax.Precision.HIGHEST)
    by_offset = jnp.where(col_ok, expanded * LOG2E, NEG)
    masked = jnp.full_like(by_offset[:, 0], NEG)
    tables = []
    for first_row in (0, n, grid_rows - n):
        key_rows = []
        for kr in range(3 * n):
            blocks = []
            for qr in range(n):
                q_abs, k_abs = first_row + qr, first_row - n + kr
                row_start = min(max(q_abs - NA_ROW_K // 2, 0), grid_rows - NA_ROW_K)
                in_window = row_start <= k_abs < row_start + NA_ROW_K
                blocks.append(by_offset[:, k_abs - q_abs + NA_ROW_K - 1] if in_window else masked)
            key_rows.append(jnp.concatenate(blocks, axis=-1))
        tables.append(jnp.concatenate(key_rows, axis=1))
    return jnp.stack(tables)


def _na_specs(s, w, bias_shape):
    blk = NA_ROWS_PER_STEP * GRID_W
    n_blk = s // blk
    clamp = lambda j: jnp.clip(j, 0, n_blk - 1)
    tok = lambda shift: pl.BlockSpec((None, blk, w), lambda bi, j: (bi, clamp(j + shift), 0))
    chan = lambda shift: pl.BlockSpec((None, w, blk), lambda bi, j: (bi, 0, clamp(j + shift)))
    table = pl.BlockSpec(
        (None,) + bias_shape[1:],
        lambda bi, j: (jnp.where(j == 0, 0, jnp.where(j == n_blk - 1, 2, 1)), 0, 0, 0))
    return [tok(0), tok(-1), tok(0), tok(1), chan(-1), chan(0), chan(1), table], tok(0)


def _sw_stage(sink_ref, q_ref, kp_ref, kc_ref, kn_ref, vp_ref, vc_ref, vn_ref, o_ref):
    i = pl.program_id(1)
    tq = q_ref.shape[0]
    win = SW_WINDOW
    k = jnp.concatenate([kp_ref[...], kc_ref[...], kn_ref[...]], axis=0)
    vt = jnp.concatenate([vp_ref[...], vc_ref[...], vn_ref[...]], axis=1)
    n_kblk = k.shape[0] // win
    key_row = lax.broadcasted_iota(jnp.int32, (win, win), 0)
    query_col = lax.broadcasted_iota(jnp.int32, (win, win), 1)
    band = {0: jnp.where(key_row >= query_col, 0.0, NEG), 1: None,
            2: jnp.where(key_row <= query_col, 0.0, NEG)}
    outside = {0: jnp.where(i == 0, NEG, 0.0),
               n_kblk - 1: jnp.where(i == pl.num_programs(1) - 1, NEG, 0.0)}
    masks = {}
    for c in range(tq // win):
        for d in range(3):
            a = c + d
            mask = band[d]
            if a in outside:
                mask = outside[a] if mask is None else mask + outside[a]
            masks[a, c] = mask
    low = _low_half_mask()
    group = SW_Q_HEADS // SW_KV_HEADS

    def scores(head):
        kv_head = head // group
        qc = q_ref[:, head // 2 * LANES:(head // 2 + 1) * LANES]
        src = qc if head % 2 == kv_head else pltpu.roll(qc, HEAD_DIM, axis=1)
        qz = jnp.where(low if kv_head == 0 else ~low, src, jnp.zeros_like(src))
        return _dot_nt(k, qz)

    def attend(head, s):
        kv_head = head // group
        sink = sink_ref[0, head]
        p_cols, l_cols = [], []
        for c in range(tq // win):
            blocks = []
            for a in range(c, c + 3):
                blk = s[a * win:(a + 1) * win, c * win:(c + 1) * win]
                blocks.append(blk if masks[a, c] is None else blk + masks[a, c])
            m = jnp.maximum(jnp.max(jnp.maximum(jnp.maximum(blocks[0], blocks[1]), blocks[2]),
                                    axis=0, keepdims=True), sink)
            l8, packed = None, []
            for blk in blocks:
                p = jnp.exp2(blk - m)
                ps = _fold8(p, jnp.sum)
                l8 = ps if l8 is None else l8 + ps
                packed.append(p.astype(BF16))
            l_cols.append(jnp.sum(l8, axis=0, keepdims=True) + jnp.exp2(sink - m))
            empty = jnp.zeros((win, win), BF16)
            p_cols.append(jnp.concatenate(
                [empty] * c + packed + [empty] * (n_kblk - c - 3), axis=0))
        p_all = jnp.concatenate(p_cols, axis=1)
        o = _dot(vt, p_all) / jnp.concatenate(l_cols, axis=1)
        return o[kv_head * HEAD_DIM:(kv_head + 1) * HEAD_DIM]

    def finish(res):
        for c in range(B_QW // LANES):
            o_ref[:, c * LANES:(c + 1) * LANES] = (
                jnp.concatenate(res[2 * c:2 * c + 2], axis=0).T.astype(o_ref.dtype))

    return scores, attend, finish


def _sw_specs(s, w, kvw):
    tq = SW_Q_TILE
    per = tq // SW_WINDOW
    n_blk = s // SW_WINDOW
    prev = lambda i: jnp.maximum(per * i - 1, 0)
    nxt = lambda i: jnp.minimum(per * (i + 1), n_blk - 1)
    q_spec = pl.BlockSpec((None, tq, w), lambda bi, i: (bi, i, 0))
    k_specs = [pl.BlockSpec((None, SW_WINDOW, kvw), lambda bi, i: (bi, prev(i), 0)),
               pl.BlockSpec((None, tq, kvw), lambda bi, i: (bi, i, 0)),
               pl.BlockSpec((None, SW_WINDOW, kvw), lambda bi, i: (bi, nxt(i), 0))]
    vt_specs = [pl.BlockSpec((None, kvw, SW_WINDOW), lambda bi, i: (bi, 0, prev(i))),
                pl.BlockSpec((None, kvw, tq), lambda bi, i: (bi, 0, i)),
                pl.BlockSpec((None, kvw, SW_WINDOW), lambda bi, i: (bi, 0, nxt(i)))]
    sink_spec = pl.BlockSpec(memory_space=pltpu.SMEM)
    return [sink_spec, q_spec] + k_specs + vt_specs, q_spec


def _local_kernel(*refs):
    n_na = 8
    na = _na_stage(*refs[:n_na], refs[-2])
    sw = _sw_stage(*refs[n_na:-2], refs[-1])
    jobs = [(stage, head) for head in range(NA_HEADS) for stage in (na, sw)]
    res = _head_pipeline(jobs, lambda job: job[0][0](job[1]),
                         lambda job, s: job[0][1](job[1], s))
    na[2](res[0::2])
    sw[2](res[1::2])


def _local_attention(qa, ka, vat, bias, qb, kb, vbt, sink):
    b, s, w = qa.shape
    assert NA_ROWS_PER_STEP * GRID_W == SW_Q_TILE and NA_HEADS == SW_Q_HEADS
    na_in, na_out = _na_specs(s, w, bias.shape)
    sw_in, sw_out = _sw_specs(s, w, kb.shape[-1])
    out = jax.ShapeDtypeStruct((b, s, w), BF16)
    return pl.pallas_call(
        _local_kernel,
        grid=(b, s // SW_Q_TILE),
        in_specs=na_in + sw_in,
        out_specs=[na_out, sw_out],
        out_shape=[out, out],
        compiler_params=pltpu.CompilerParams(
            dimension_semantics=("arbitrary", "arbitrary"), vmem_limit_bytes=40 * MIB),
        name="local_attn",
    )(qa, ka, ka, ka, vat, vat, vat, bias, sink, qb, kb, kb, kb, vbt, vbt, vbt)


def _diff_kernel(q_ref, k_ref, vt_ref, lq1_ref, lk1_ref, lq2_ref, lk2_ref, g_ref, o_ref,
                 m_ref, *scratch, lam_init):
    tq = DIFF_Q_TILE
    n_sub = q_ref.shape[0] // tq
    per = len(scratch) // n_sub
    subs = [scratch[i * per:(i + 1) * per] for i in range(n_sub)]
    kv_tile = subs[0][2].shape[0]
    n_kv = k_ref.shape[0] // kv_tile
    cols = 2 * tq
    low = _low_half_mask()

    lam = (jnp.exp(jnp.sum(lq1_ref[...] * lk1_ref[...]))
           - jnp.exp(jnp.sum(lq2_ref[...] * lk2_ref[...])) + lam_init)

    def finalize(sub):
        l_ref, acc_ref = subs[sub][:2]
        o = acc_ref[...] / l_ref[...]
        o = o[:, :tq] - lam * o[:, tq:]
        o = o * lax.rsqrt(jnp.mean(o * o, axis=0, keepdims=True) + EPS) * g_ref[...]
        o_ref[sub * tq:(sub + 1) * tq, :] = (o * (1.0 - lam_init)).T.astype(o_ref.dtype)

    def fast_sweep(sub):
        l_ref, acc_ref, p0_ref, p1_ref, qz_ref = subs[sub]
        p_bufs = (p0_ref, p1_ref)
        q = q_ref[sub * tq:(sub + 1) * tq, :]
        zero = jnp.zeros_like(q)
        qz_ref[0:tq, :] = jnp.where(low, q, zero)
        qz_ref[tq:, :] = jnp.where(low, zero, q)
        ref = None
        unsafe = jnp.zeros((1, cols), F32)
        l8 = None
        beta = None
        pending = None

        def value_product(tile, scale):
            pv = _dot(vt_ref[:, tile * kv_tile:(tile + 1) * kv_tile], p_bufs[tile % 2][...])
            acc_ref[...] = pv if scale is None else scale * acc_ref[...] + pv

        for t in range(n_kv):
            tile_sum8 = None
            s = _dot_nt(k_ref[t * kv_tile:(t + 1) * kv_tile, :], qz_ref[...])
            if ref is None:
                ref = _col_max(_fold8(s[0:DIFF_WARM_ROWS], jnp.max))
            if pending is not None:
                value_product(*pending)
            for r in range(0, kv_tile, DIFF_PIECE_ROWS):
                p = jnp.exp2(s[r:r + DIFF_PIECE_ROWS] - ref)
                ps = _fold8(p, jnp.sum)
                tile_sum8 = ps if tile_sum8 is None else tile_sum8 + ps
                p_bufs[t % 2][r:r + DIFF_PIECE_ROWS, :] = p.astype(BF16)
            pending = (t, beta)
            l8 = tile_sum8 if beta is None else beta * l8 + tile_sum8
            grow = jnp.maximum(jnp.sum(tile_sum8, axis=0, keepdims=True), 1.0)
            unsafe = jnp.maximum(unsafe, jnp.where(grow > DIFF_SUM_LIMIT, 1.0, 0.0))
            beta = 1.0 / grow
            ref = ref + jnp.log2(grow)
        value_product(*pending)
        l_fast = jnp.sum(l8, axis=0, keepdims=True)
        l_ref[...] = l_fast
        finite = jnp.isfinite(jnp.sum(acc_ref[...], axis=0, keepdims=True) + l_fast)
        return jnp.maximum(unsafe, jnp.where(finite, 0.0, 1.0))

    def exact_sweep(sub):
        l_ref, acc_ref, _, _, qz_ref = subs[sub]
        m_ref[...] = jnp.full(m_ref.shape, -jnp.inf, F32)
        l_ref[...] = jnp.zeros(l_ref.shape, F32)
        acc_ref[...] = jnp.zeros(acc_ref.shape, F32)

        def tile_step(j, carry):
            rows = pl.ds(pl.multiple_of(j * kv_tile, kv_tile), kv_tile)
            s = _dot_nt(k_ref[rows, :], qz_ref[...])
            m_old = m_ref[...]
            m_new = jnp.maximum(m_old, jnp.max(s, axis=0, keepdims=True))
            alpha = jnp.exp2(m_old - m_new)
            p = jnp.exp2(s - m_new)
            m_ref[...] = m_new
            l_ref[...] = alpha * l_ref[...] + jnp.sum(p, axis=0, keepdims=True)
            acc_ref[...] = alpha * acc_ref[...] + _dot(vt_ref[:, rows], p.astype(BF16))
            return carry

        lax.fori_loop(0, n_kv, tile_step, 0)

    unsafe = None
    for sub in range(n_sub):
        flag = fast_sweep(sub)
        unsafe = flag if unsafe is None else jnp.maximum(unsafe, flag)
        finalize(sub)

    @pl.when(jnp.max(unsafe) > 0.0)
    def _():
        for sub in range(n_sub):
            exact_sweep(sub)
            finalize(sub)


def _diff(q, k, vt, lq1, lk1, lq2, lk2, subln_g, lam_init):
    b, s, w = q.shape
    tq = DIFF_Q_TILE
    dv = 2 * HEAD_DIM
    heads = w // LANES
    kv_tile = min(DIFF_KV_TILE, s)
    n_sub = min(DIFF_Q_SUBTILES, s // tq)
    q_spec = pl.BlockSpec((None, n_sub * tq, LANES), lambda bi, h, i: (bi, i, h))
    k_spec = pl.BlockSpec((None, s, LANES), lambda bi, h, i: (bi, 0, h))
    vt_spec = pl.BlockSpec((None, dv, s), lambda bi, h, i: (bi, h, 0))
    vec = lambda n: pl.BlockSpec((1, n), lambda bi, h, i: (0, 0))
    sub_scratch = [pltpu.VMEM((1, 2 * tq), F32), pltpu.VMEM((dv, 2 * tq), F32),
                   pltpu.VMEM((kv_tile, 2 * tq), BF16), pltpu.VMEM((kv_tile, 2 * tq), BF16),
                   pltpu.VMEM((2 * tq, LANES), BF16)]
    return pl.pallas_call(
        functools.partial(_diff_kernel, lam_init=lam_init),
        grid=(b, heads, s // (n_sub * tq)),
        in_specs=[q_spec, k_spec, vt_spec, vec(HEAD_DIM), vec(HEAD_DIM), vec(HEAD_DIM),
                  vec(HEAD_DIM), pl.BlockSpec((dv, 1), lambda bi, h, i: (0, 0))],
        out_specs=q_spec,
        out_shape=jax.ShapeDtypeStruct((b, s, w), BF16),
        scratch_shapes=[pltpu.VMEM((1, 2 * tq), F32)] + sub_scratch * n_sub,
        compiler_params=pltpu.CompilerParams(
            dimension_semantics=("arbitrary", "arbitrary", "arbitrary"),
            vmem_limit_bytes=32 * MIB),
        name="differential_attn",
    )(q, k, vt, lq1, lk1, lq2, lk2, subln_g)


def _merge_kernel(h_ref, u_ref, oa_ref, ob_ref, oc_ref, wg_ref, bg_ref, wb_ref, wo_ref, g_ref,
                  o_ref):
    d = h_ref.shape[1]
    for r0 in range(0, h_ref.shape[0], SUB_ROWS):
        rows = slice(r0, r0 + SUB_ROWS)
        u = u_ref[rows, :]
        merged = None
        for n, br_ref in enumerate((oa_ref, ob_ref, oc_ref)):
            gate = jax.nn.sigmoid(_dot(u, wg_ref[:, n * d:(n + 1) * d]) + bg_ref[n:n + 1, :])
            term = gate * _dot(br_ref[rows, :], wb_ref[n])
            merged = term if merged is None else merged + term
        y = _dot(merged.astype(BF16), wo_ref[...])
        o_ref[rows, :] = h_ref[rows, :] + _rmsnorm(y, g_ref[...])


def _merge(h, u, oa, ob, oc, w_gate, b_gate, w_branch, w_out, post_g):
    t, d = h.shape
    rows = lambda w: pl.BlockSpec((TOKEN_TILE, w), lambda i: (i, 0))
    return pl.pallas_call(
        _merge_kernel,
        grid=(t // TOKEN_TILE,),
        in_specs=[rows(d), rows(d), rows(oa.shape[1]), rows(ob.shape[1]), rows(oc.shape[1]),
                  _resident(w_gate.shape), _resident(b_gate.shape), _resident(w_branch.shape),
                  _resident(w_out.shape), _resident((1, d))],
        out_specs=rows(d),
        out_shape=jax.ShapeDtypeStruct((t, d), F32),
        compiler_params=pltpu.CompilerParams(
            dimension_semantics=("arbitrary",), vmem_limit_bytes=48 * MIB),
        name="gated_merge",
    )(h, u, oa, ob, oc, w_gate, b_gate, w_branch, w_out, post_g)


def _rope_tables(s):
    pos = jnp.arange(s, dtype=F32)
    inv = ROPE_THETA ** (-jnp.arange(0, HEAD_DIM, 2, dtype=F32) / HEAD_DIM)
    ang = pos[:, None] * inv[None, :]
    cos = jnp.tile(jnp.cos(ang), (1, 2 * LANES // HEAD_DIM))
    sin = jnp.tile(jnp.sin(ang), (1, 2 * LANES // HEAD_DIM))
    first_half = (jnp.arange(LANES) % HEAD_DIM) < HEAD_DIM // 2
    return cos, jnp.where(first_half, -sin, 0.0), jnp.where(first_half, 0.0, sin)


def kernel(x, ffn1_pre_g, ffn1_w_gu, ffn1_w_down, ffn1_post_g, mix_pre_g, w_in, na_rpb, sw_sink, diff_lambda_q1, diff_lambda_k1, diff_lambda_q2, diff_lambda_k2, diff_subln_g, w_branch, b_gate, w_out, mix_post_g, ffn2_pre_g, ffn2_w_gu, ffn2_w_down, ffn2_post_g):
    b, s, d = x.shape
    depth = w_in.shape[0]
    assert s % (NA_ROWS_PER_STEP * GRID_W) == 0 and s % TOKEN_TILE == 0 and s >= NA_ROW_K * GRID_W
    assert (b * s) % FFN_TILE == 0
    cos, sin_lo, sin_hi = _rope_tables(s)
    row = lambda a: a.reshape(1, -1).astype(F32)
    h = x.reshape(b * s, d)
    for l in range(depth):
        h = _ffn(h, row(ffn1_pre_g[l]), ffn1_w_gu[l].astype(BF16), ffn1_w_down[l].astype(BF16),
                 row(ffn1_post_g[l]))

        w_l = w_in[l]
        qa_w, ka_w, va_w, qb_w, kb_w, vb_w, qc_w, kc_w, vc_w, gate_w = (
            w_l[:, c0:c1] for c0, c1 in zip(_SPLIT_EDGES[:-1], _SPLIT_EDGES[1:]))
        w_qk = jnp.concatenate([qa_w, ka_w, qb_w, kb_w, qc_w, kc_w], axis=1).astype(BF16)
        w_vt = jnp.concatenate([va_w, vb_w, vc_w], axis=1).T.astype(BF16)
        u, qa, ka, qb, kb, qc, kc, vat, vbt, vct = _proj(
            h, row(mix_pre_g[l]), w_qk, w_vt, cos, sin_lo, sin_hi)
        seq = lambda a: a.reshape(b, s, a.shape[-1])
        oa, ob = _local_attention(
            seq(qa), seq(ka), vat, _na_bias_table(na_rpb[l], s // GRID_W),
            seq(qb), seq(kb), vbt, sw_sink[l].reshape(1, -1).astype(F32) * LOG2E)
        lam_init = 0.8 - 0.6 * math.exp(-0.3 * l)
        oc = _diff(seq(qc), seq(kc), vct, row(diff_lambda_q1[l]), row(diff_lambda_k1[l]),
                   row(diff_lambda_q2[l]), row(diff_lambda_k2[l]),
                   diff_subln_g[l].reshape(-1, 1).astype(F32), lam_init)
        flat = lambda a: a.reshape(b * s, a.shape[-1])
        h = _merge(h, u, flat(oa), flat(ob), flat(oc), gate_w.astype(BF16), b_gate[l].astype(F32),
                   w_branch[l].astype(BF16), w_out[l].astype(BF16), row(mix_post_g[l]))

        h = _ffn(h, row(ffn2_pre_g[l]), ffn2_w_gu[l].astype(BF16), ffn2_w_down[l].astype(BF16),
                 row(ffn2_post_g[l]))
    return h.reshape(b, s, d)
```

```python
import functools
import math

import jax
import jax.numpy as jnp
import numpy as np
from jax import lax
from jax.experimental import pallas as pl
from jax.experimental.pallas import tpu as pltpu

F32 = jnp.float32
BF16 = jnp.bfloat16

HEAD_DIM = 64
GRID_W = 64
NA_HEADS = 8
NA_ROW_K = 8
NA_COL_K = 16
SW_Q_HEADS = 8
SW_KV_HEADS = 2
SW_WINDOW = 128
DIFF_HEADS = 4
N_BRANCH = 3
ROPE_THETA = 10000.0
EPS = 1e-6
LOG2E = math.log2(math.e)
NEG = -1e30
LANES = 128
SUBLANES = 8
MIB = 1024 * 1024

A_W = NA_HEADS * HEAD_DIM
B_QW = SW_Q_HEADS * HEAD_DIM
B_KVW = SW_KV_HEADS * HEAD_DIM
C_W = 2 * DIFF_HEADS * HEAD_DIM
_SPLIT_EDGES = (0, A_W, 2 * A_W, 3 * A_W, 3 * A_W + B_QW, 3 * A_W + B_QW + B_KVW,
                3 * A_W + B_QW + 2 * B_KVW, 3 * A_W + B_QW + 2 * B_KVW + C_W,
                3 * A_W + B_QW + 2 * B_KVW + 2 * C_W, 3 * A_W + B_QW + 2 * B_KVW + 3 * C_W, None)

TOKEN_TILE = 1024
FFN_TILE = 1024
SUB_ROWS = 256
NA_ROWS_PER_STEP = 4
SW_Q_TILE = 256
DIFF_Q_TILE = 256
DIFF_KV_TILE = 512
DIFF_Q_SUBTILES = 4
DIFF_PIECE_ROWS = 16
DIFF_WARM_ROWS = 128
DIFF_SUM_LIMIT = 2.0 ** 40
HEAD_PIPELINE_DEPTH = 4


def _rmsnorm(x, g):
    return x * lax.rsqrt(jnp.mean(x * x, axis=-1, keepdims=True) + EPS) * g


def _dot(a, b):
    return jnp.dot(a, b, preferred_element_type=F32)


def _dot_nt(a, b):
    return lax.dot_general(a, b, (((1,), (1,)), ((), ())), preferred_element_type=F32)


def _resident(shape):
    return pl.BlockSpec(shape, lambda *_: (0,) * len(shape), pipeline_mode=pl.Buffered(1))


def _low_half_mask():
    return lax.broadcasted_iota(jnp.int32, (1, LANES), 1) < HEAD_DIM


def _fold8(x, op):
    return op(x.reshape(x.shape[0] // SUBLANES, SUBLANES, x.shape[1]), axis=0)


def _col_max(x8):
    return jnp.max(x8, axis=0, keepdims=True)


def _head_pipeline(heads, scores, attend, depth=HEAD_PIPELINE_DEPTH):
    heads = list(heads)
    ready = [scores(h) for h in heads[:depth]]
    out = []
    for idx, head in enumerate(heads):
        if idx + depth < len(heads):
            ready.append(scores(heads[idx + depth]))
        out.append(attend(head, ready.pop(0)))
    return out


def _ffn_kernel(x_ref, pre_g_ref, wgu_ref, wdown_ref, post_g_ref, o_ref, *, f_chunks, d_ff):
    for r0 in range(0, x_ref.shape[0], SUB_ROWS):
        rows = slice(r0, r0 + SUB_ROWS)
        x = x_ref[rows, :]
        xn = _rmsnorm(x, pre_g_ref[...]).astype(BF16)
        acc = None
        for c0, c1 in f_chunks:
            gate = _dot(xn, wgu_ref[:, c0:c1])
            up = _dot(xn, wgu_ref[:, d_ff + c0:d_ff + c1])
            act = (gate * jax.nn.sigmoid(gate) * up).astype(BF16)
            part = _dot(act, wdown_ref[c0:c1, :])
            acc = part if acc is None else acc + part
        o_ref[rows, :] = x + 0.5 * _rmsnorm(acc, post_g_ref[...])


def _ffn(h, pre_g, w_gu, w_down, post_g):
    t, d = h.shape
    d_ff = w_down.shape[0]
    step = 1024
    f_chunks = tuple((c, min(c + step, d_ff)) for c in range(0, d_ff, step))
    row = pl.BlockSpec((FFN_TILE, d), lambda i: (i, 0))
    return pl.pallas_call(
        functools.partial(_ffn_kernel, f_chunks=f_chunks, d_ff=d_ff),
        grid=(t // FFN_TILE,),
        in_specs=[row, _resident((1, d)), _resident(w_gu.shape), _resident(w_down.shape),
                  _resident((1, d))],
        out_specs=row,
        out_shape=jax.ShapeDtypeStruct((t, d), F32),
        compiler_params=pltpu.CompilerParams(
            dimension_semantics=("arbitrary",), vmem_limit_bytes=48 * MIB),
        name="ffn",
    )(h, pre_g, w_gu, w_down, post_g)


def _rope(t, cos, sin_lo, sin_hi):
    out = []
    for c in range(t.shape[1] // LANES):
        tc = t[:, c * LANES:(c + 1) * LANES]
        fwd = pltpu.roll(tc, LANES - HEAD_DIM // 2, axis=1)
        back = pltpu.roll(tc, HEAD_DIM // 2, axis=1)
        out.append(tc * cos + fwd * sin_lo + back * sin_hi)
    return jnp.concatenate(out, axis=1)


def _proj_kernel(h_ref, g_ref, wqk_ref, wvt_ref, cos_ref, sin_lo_ref, sin_hi_ref,
                 u_ref, qa_ref, ka_ref, qb_ref, kb_ref, qc_ref, kc_ref, vat_ref, vbt_ref, vct_ref):
    scale = HEAD_DIM ** -0.5 * LOG2E
    for r0 in range(0, h_ref.shape[0], SUB_ROWS):
        rows = slice(r0, r0 + SUB_ROWS)
        un = _rmsnorm(h_ref[rows, :], g_ref[...]).astype(BF16)
        u_ref[rows, :] = un
        cos, sin_lo, sin_hi = cos_ref[rows, :], sin_lo_ref[rows, :], sin_hi_ref[rows, :]
        col = 0
        for out_ref, rotary, scaled in ((qa_ref, False, True), (ka_ref, False, False),
                                        (qb_ref, True, True), (kb_ref, True, False),
                                        (qc_ref, True, True), (kc_ref, True, False)):
            width = out_ref.shape[1]
            t = _dot(un, wqk_ref[:, col:col + width])
            col += width
            if rotary:
                t = _rope(t, cos, sin_lo, sin_hi)
            out_ref[rows, :] = (t * scale if scaled else t).astype(BF16)
        row = 0
        for vt_ref in (vat_ref, vbt_ref, vct_ref):
            width = vt_ref.shape[0]
            vt_ref[:, rows] = _dot_nt(wvt_ref[row:row + width, :], un).astype(BF16)
            row += width


def _proj(h, g, w_qk, w_vt, cos, sin_lo, sin_hi):
    t, d = h.shape
    s = cos.shape[0]
    tiles_per_seq = s // TOKEN_TILE
    widths = (d, A_W, A_W, B_QW, B_KVW, C_W, C_W)
    v_widths = (A_W, B_KVW, C_W)
    rows = lambda w: pl.BlockSpec((TOKEN_TILE, w), lambda i: (i, 0))
    chans = lambda w: pl.BlockSpec((None, w, TOKEN_TILE),
                                   lambda i: (i // tiles_per_seq, 0, i % tiles_per_seq))
    table = pl.BlockSpec((TOKEN_TILE, LANES), lambda i: (i % tiles_per_seq, 0))
    return pl.pallas_call(
        _proj_kernel,
        grid=(t // TOKEN_TILE,),
        in_specs=[rows(d), _resident((1, d)), _resident(w_qk.shape), _resident(w_vt.shape),
                  table, table, table],
        out_specs=[rows(w) for w in widths] + [chans(w) for w in v_widths],
        out_shape=([jax.ShapeDtypeStruct((t, w), BF16) for w in widths]
                   + [jax.ShapeDtypeStruct((t // s, w, s), BF16) for w in v_widths]),
        compiler_params=pltpu.CompilerParams(
            dimension_semantics=("arbitrary",), vmem_limit_bytes=48 * MIB),
        name="proj",
    )(h, g, w_qk, w_vt, cos, sin_lo, sin_hi)


def _na_stage(q_ref, kp_ref, kc_ref, kn_ref, vtp_ref, vtc_ref, vtn_ref, bias_ref, o_ref):
    blk = q_ref.shape[0]
    k_refs = (kp_ref, kc_ref, kn_ref)
    vt_refs = (vtp_ref, vtc_ref, vtn_ref)
    low = _low_half_mask()
    top = lax.broadcasted_iota(jnp.int32, (LANES, 1), 0) < HEAD_DIM
    lanes = lambda head: slice(head // 2 * LANES, (head // 2 + 1) * LANES)

    def scores(head):
        qc = q_ref[:, lanes(head)]
        qz = jnp.where(low if head % 2 == 0 else ~low, qc, jnp.zeros_like(qc))
        return [_dot_nt(k_refs[w][:, lanes(head)], qz) + bias_ref[head, w * blk:(w + 1) * blk, :]
                for w in range(3)]

    ones_rows = jnp.ones((2 * SUBLANES, blk), BF16)
    vt_ones = {}

    def values_and_ones(w, chunk):
        if (w, chunk) not in vt_ones:
            vt_ones[w, chunk] = jnp.concatenate(
                [vt_refs[w][chunk * LANES:(chunk + 1) * LANES, :], ones_rows], axis=0)
        return vt_ones[w, chunk]

    def attend(head, s):
        m = jnp.max(jnp.maximum(jnp.maximum(s[0], s[1]), s[2]), axis=0, keepdims=True)
        o = None
        for w in range(3):
            pv = _dot(values_and_ones(w, head // 2), jnp.exp2(s[w] - m).astype(BF16))
            o = pv if o is None else o + pv
        return o[:LANES] / o[LANES:LANES + 1]

    def finish(res):
        for c in range(A_W // LANES):
            o_ref[:, c * LANES:(c + 1) * LANES] = (
                jnp.where(top, res[2 * c], res[2 * c + 1]).T.astype(o_ref.dtype))

    return scores, attend, finish


def _na_bias_table(rpb, grid_rows):
    n = NA_ROWS_PER_STEP
    c = np.arange(GRID_W)
    dc = np.clip(c[:, None] - c[None, :] + (NA_COL_K - 1), 0, 2 * NA_COL_K - 2)
    col_start = np.clip(c - NA_COL_K // 2, 0, GRID_W - NA_COL_K)
    col_ok = (c[:, None] >= col_start[None, :]) & (c[:, None] < col_start[None, :] + NA_COL_K)
    one_hot = jnp.asarray(dc[None] == np.arange(2 * NA_COL_K - 1)[:, None, None], F32)
    expanded = jnp.einsum("hdj,jkq->hdkq", rpb.astype(F32), one_hot,
                          precision=lax.Precision.HIGHEST)
    by_offset = jnp.where(col_ok, expanded * LOG2E, NEG)
    masked = jnp.full_like(by_offset[:, 0], NEG)
    tables = []
    for first_row in (0, n, grid_rows - n):
        key_rows = []
        for kr in range(3 * n):
            blocks = []
            for qr in range(n):
                q_abs, k_abs = first_row + qr, first_row - n + kr
                row_start = min(max(q_abs - NA_ROW_K // 2, 0), grid_rows - NA_ROW_K)
                in_window = row_start <= k_abs < row_start + NA_ROW_K
                blocks.append(by_offset[:, k_abs - q_abs + NA_ROW_K - 1] if in_window else masked)
            key_rows.append(jnp.concatenate(blocks, axis=-1))
        tables.append(jnp.concatenate(key_rows, axis=1))
    return jnp.stack(tables)


def _na_specs(s, w, bias_shape):
    blk = NA_ROWS_PER_STEP * GRID_W
    n_blk = s // blk
    clamp = lambda j: jnp.clip(j, 0, n_blk - 1)
    tok = lambda shift: pl.BlockSpec((None, blk, w), lambda bi, j: (bi, clamp(j + shift), 0))
    chan = lambda shift: pl.BlockSpec((None, w, blk), lambda bi, j: (bi, 0, clamp(j + shift)))
    table = pl.BlockSpec(
        (None,) + bias_shape[1:],
        lambda bi, j: (jnp.where(j == 0, 0, jnp.where(j == n_blk - 1, 2, 1)), 0, 0, 0))
    return [tok(0), tok(-1), tok(0), tok(1), chan(-1), chan(0), chan(1), table], tok(0)


def _sw_stage(sink_ref, q_ref, kp_ref, kc_ref, kn_ref, vp_ref, vc_ref, vn_ref, o_ref):
    i = pl.program_id(1)
    tq = q_ref.shape[0]
    win = SW_WINDOW
    k = jnp.concatenate([kp_ref[...], kc_ref[...], kn_ref[...]], axis=0)
    vt = jnp.concatenate([vp_ref[...], vc_ref[...], vn_ref[...]], axis=1)
    vt_ones = jnp.concatenate([vt, jnp.ones((2 * SUBLANES, vt.shape[1]), BF16)], axis=0)
    n_kblk = k.shape[0] // win
    key_row = lax.broadcasted_iota(jnp.int32, (win, win), 0)
    query_col = lax.broadcasted_iota(jnp.int32, (win, win), 1)
    band = {0: jnp.where(key_row >= query_col, 0.0, NEG), 1: None,
            2: jnp.where(key_row <= query_col, 0.0, NEG)}
    outside = {0: jnp.where(i == 0, NEG, 0.0),
               n_kblk - 1: jnp.where(i == pl.num_programs(1) - 1, NEG, 0.0)}
    masks = {}
    for c in range(tq // win):
        for d in range(3):
            a = c + d
            mask = band[d]
            if a in outside:
                mask = outside[a] if mask is None else mask + outside[a]
            masks[a, c] = mask
    low = _low_half_mask()
    group = SW_Q_HEADS // SW_KV_HEADS

    def scores(head):
        kv_head = head // group
        qc = q_ref[:, head // 2 * LANES:(head // 2 + 1) * LANES]
        src = qc if head % 2 == kv_head else pltpu.roll(qc, HEAD_DIM, axis=1)
        qz = jnp.where(low if kv_head == 0 else ~low, src, jnp.zeros_like(src))
        return _dot_nt(k, qz)

    def attend(head, s):
        kv_head = head // group
        sink = sink_ref[0, head]
        p_cols, l_cols = [], []
        for c in range(tq // win):
            blocks = []
            for a in range(c, c + 3):
                blk = s[a * win:(a + 1) * win, c * win:(c + 1) * win]
                blocks.append(blk if masks[a, c] is None else blk + masks[a, c])
            m = jnp.maximum(jnp.max(jnp.maximum(jnp.maximum(blocks[0], blocks[1]), blocks[2]),
                                    axis=0, keepdims=True), sink)
            packed = [jnp.exp2(blk - m).astype(BF16) for blk in blocks]
            l_cols.append(jnp.exp2(sink - m))
            empty = jnp.zeros((win, win), BF16)
            p_cols.append(jnp.concatenate(
                [empty] * c + packed + [empty] * (n_kblk - c - 3), axis=0))
        p_all = jnp.concatenate(p_cols, axis=1)
        o = _dot(vt_ones, p_all)
        o = o[:LANES] / (o[LANES:LANES + 1] + jnp.concatenate(l_cols, axis=1))
        return o[kv_head * HEAD_DIM:(kv_head + 1) * HEAD_DIM]

    def finish(res):
        for c in range(B_QW // LANES):
            o_ref[:, c * LANES:(c + 1) * LANES] = (
                jnp.concatenate(res[2 * c:2 * c + 2], axis=0).T.astype(o_ref.dtype))

    return scores, attend, finish


def _sw_specs(s, w, kvw):
    tq = SW_Q_TILE
    per = tq // SW_WINDOW
    n_blk = s // SW_WINDOW
    prev = lambda i: jnp.maximum(per * i - 1, 0)
    nxt = lambda i: jnp.minimum(per * (i + 1), n_blk - 1)
    q_spec = pl.BlockSpec((None, tq, w), lambda bi, i: (bi, i, 0))
    k_specs = [pl.BlockSpec((None, SW_WINDOW, kvw), lambda bi, i: (bi, prev(i), 0)),
               pl.BlockSpec((None, tq, kvw), lambda bi, i: (bi, i, 0)),
               pl.BlockSpec((None, SW_WINDOW, kvw), lambda bi, i: (bi, nxt(i), 0))]
    vt_specs = [pl.BlockSpec((None, kvw, SW_WINDOW), lambda bi, i: (bi, 0, prev(i))),
                pl.BlockSpec((None, kvw, tq), lambda bi, i: (bi, 0, i)),
                pl.BlockSpec((None, kvw, SW_WINDOW), lambda bi, i: (bi, 0, nxt(i)))]
    sink_spec = pl.BlockSpec(memory_space=pltpu.SMEM)
    return [sink_spec, q_spec] + k_specs + vt_specs, q_spec


def _local_kernel(*refs):
    n_na = 8
    na = _na_stage(*refs[:n_na], refs[-2])
    sw = _sw_stage(*refs[n_na:-2], refs[-1])
    jobs = [(stage, head) for head in range(NA_HEADS) for stage in (na, sw)]
    res = _head_pipeline(jobs, lambda job: job[0][0](job[1]),
                         lambda job, s: job[0][1](job[1], s))
    na[2](res[0::2])
    sw[2](res[1::2])


def _local_attention(qa, ka, vat, bias, qb, kb, vbt, sink):
    b, s, w = qa.shape
    assert NA_ROWS_PER_STEP * GRID_W == SW_Q_TILE and NA_HEADS == SW_Q_HEADS
    na_in, na_out = _na_specs(s, w, bias.shape)
    sw_in, sw_out = _sw_specs(s, w, kb.shape[-1])
    out = jax.ShapeDtypeStruct((b, s, w), BF16)
    return pl.pallas_call(
        _local_kernel,
        grid=(b, s // SW_Q_TILE),
        in_specs=na_in + sw_in,
        out_specs=[na_out, sw_out],
        out_shape=[out, out],
        compiler_params=pltpu.CompilerParams(
            dimension_semantics=("arbitrary", "arbitrary"), vmem_limit_bytes=40 * MIB),
        name="local_attn",
    )(qa, ka, ka, ka, vat, vat, vat, bias, sink, qb, kb, kb, kb, vbt, vbt, vbt)


def _diff_kernel(q_ref, k_ref, vt_ref, lq1_ref, lk1_ref, lq2_ref, lk2_ref, g_ref, o_ref,
                 m_ref, *scratch, lam_init):
    tq = DIFF_Q_TILE
    n_sub = q_ref.shape[0] // tq
    per = len(scratch) // n_sub
    subs = [scratch[i * per:(i + 1) * per] for i in range(n_sub)]
    kv_tile = subs[0][2].shape[0]
    n_kv = k_ref.shape[0] // kv_tile
    cols = 2 * tq
    low = _low_half_mask()

    lam = (jnp.exp(jnp.sum(lq1_ref[...] * lk1_ref[...]))
           - jnp.exp(jnp.sum(lq2_ref[...] * lk2_ref[...])) + lam_init)

    def finalize(sub):
        l_ref, acc_ref = subs[sub][:2]
        o = acc_ref[...] / l_ref[...]
        o = o[:, :tq] - lam * o[:, tq:]
        o = o * lax.rsqrt(jnp.mean(o * o, axis=0, keepdims=True) + EPS) * g_ref[...]
        o_ref[sub * tq:(sub + 1) * tq, :] = (o * (1.0 - lam_init)).T.astype(o_ref.dtype)

    def fast_sweep(sub):
        l_ref, acc_ref, p0_ref, p1_ref, qz_ref = subs[sub]
        p_bufs = (p0_ref, p1_ref)
        q = q_ref[sub * tq:(sub + 1) * tq, :]
        zero = jnp.zeros_like(q)
        qz_ref[0:tq, :] = jnp.where(low, q, zero)
        qz_ref[tq:, :] = jnp.where(low, zero, q)
        ref = None
        unsafe = jnp.zeros((1, cols), F32)
        l8 = None
        beta = None
        pending = None

        def value_product(tile, scale):
            pv = _dot(vt_ref[:, tile * kv_tile:(tile + 1) * kv_tile], p_bufs[tile % 2][...])
            acc_ref[...] = pv if scale is None else scale * acc_ref[...] + pv

        for t in range(n_kv):
            tile_sum8 = None
            s = _dot_nt(k_ref[t * kv_tile:(t + 1) * kv_tile, :], qz_ref[...])
            if ref is None:
                ref = _col_max(_fold8(s[0:DIFF_WARM_ROWS], jnp.max))
            if pending is not None:
                value_product(*pending)
            for r in range(0, kv_tile, DIFF_PIECE_ROWS):
                p = jnp.exp2(s[r:r + DIFF_PIECE_ROWS] - ref)
                ps = _fold8(p, jnp.sum)
                tile_sum8 = ps if tile_sum8 is None else tile_sum8 + ps
                p_bufs[t % 2][r:r + DIFF_PIECE_ROWS, :] = p.astype(BF16)
            pending = (t, beta)
            l8 = tile_sum8 if beta is None else beta * l8 + tile_sum8
            grow = jnp.maximum(jnp.sum(tile_sum8, axis=0, keepdims=True), 1.0)
            unsafe = jnp.maximum(unsafe, jnp.where(grow > DIFF_SUM_LIMIT, 1.0, 0.0))
            beta = 1.0 / grow
            ref = ref + jnp.log2(grow)
        value_product(*pending)
        l_fast = jnp.sum(l8, axis=0, keepdims=True)
        l_ref[...] = l_fast
        finite = jnp.isfinite(jnp.sum(acc_ref[...], axis=0, keepdims=True) + l_fast)
        return jnp.maximum(unsafe, jnp.where(finite, 0.0, 1.0))

    def exact_sweep(sub):
        l_ref, acc_ref, _, _, qz_ref = subs[sub]
        m_ref[...] = jnp.full(m_ref.shape, -jnp.inf, F32)
        l_ref[...] = jnp.zeros(l_ref.shape, F32)
        acc_ref[...] = jnp.zeros(acc_ref.shape, F32)

        def tile_step(j, carry):
            rows = pl.ds(pl.multiple_of(j * kv_tile, kv_tile), kv_tile)
            s = _dot_nt(k_ref[rows, :], qz_ref[...])
            m_old = m_ref[...]
            m_new = jnp.maximum(m_old, jnp.max(s, axis=0, keepdims=True))
            alpha = jnp.exp2(m_old - m_new)
            p = jnp.exp2(s - m_new)
            m_ref[...] = m_new
            l_ref[...] = alpha * l_ref[...] + jnp.sum(p, axis=0, keepdims=True)
            acc_ref[...] = alpha * acc_ref[...] + _dot(vt_ref[:, rows], p.astype(BF16))
            return carry

        lax.fori_loop(0, n_kv, tile_step, 0)

    unsafe = None
    for sub in range(n_sub):
        flag = fast_sweep(sub)
        unsafe = flag if unsafe is None else jnp.maximum(unsafe, flag)
        finalize(sub)

    @pl.when(jnp.max(unsafe) > 0.0)
    def _():
        for sub in range(n_sub):
            exact_sweep(sub)
            finalize(sub)


def _diff(q, k, vt, lq1, lk1, lq2, lk2, subln_g, lam_init):
    b, s, w = q.shape
    tq = DIFF_Q_TILE
    dv = 2 * HEAD_DIM
    heads = w // LANES
    kv_tile = min(DIFF_KV_TILE, s)
    n_sub = min(DIFF_Q_SUBTILES, s // tq)
    q_spec = pl.BlockSpec((None, n_sub * tq, LANES), lambda bi, h, i: (bi, i, h))
    k_spec = pl.BlockSpec((None, s, LANES), lambda bi, h, i: (bi, 0, h))
    vt_spec = pl.BlockSpec((None, dv, s), lambda bi, h, i: (bi, h, 0))
    vec = lambda n: pl.BlockSpec((1, n), lambda bi, h, i: (0, 0))
    sub_scratch = [pltpu.VMEM((1, 2 * tq), F32), pltpu.VMEM((dv, 2 * tq), F32),
                   pltpu.VMEM((kv_tile, 2 * tq), BF16), pltpu.VMEM((kv_tile, 2 * tq), BF16),
                   pltpu.VMEM((2 * tq, LANES), BF16)]
    return pl.pallas_call(
        functools.partial(_diff_kernel, lam_init=lam_init),
        grid=(b, heads, s // (n_sub * tq)),
        in_specs=[q_spec, k_spec, vt_spec, vec(HEAD_DIM), vec(HEAD_DIM), vec(HEAD_DIM),
                  vec(HEAD_DIM), pl.BlockSpec((dv, 1), lambda bi, h, i: (0, 0))],
        out_specs=q_spec,
        out_shape=jax.ShapeDtypeStruct((b, s, w), BF16),
        scratch_shapes=[pltpu.VMEM((1, 2 * tq), F32)] + sub_scratch * n_sub,
        compiler_params=pltpu.CompilerParams(
            dimension_semantics=("arbitrary", "arbitrary", "arbitrary"),
            vmem_limit_bytes=32 * MIB),
        name="differential_attn",
    )(q, k, vt, lq1, lk1, lq2, lk2, subln_g)


def _merge_kernel(h_ref, u_ref, oa_ref, ob_ref, oc_ref, wg_ref, bg_ref, wb_ref, wo_ref, g_ref,
                  o_ref):
    d = h_ref.shape[1]
    for r0 in range(0, h_ref.shape[0], SUB_ROWS):
        rows = slice(r0, r0 + SUB_ROWS)
        u = u_ref[rows, :]
        merged = None
        for n, br_ref in enumerate((oa_ref, ob_ref, oc_ref)):
            gate = jax.nn.sigmoid(_dot(u, wg_ref[:, n * d:(n + 1) * d]) + bg_ref[n:n + 1, :])
            term = gate * _dot(br_ref[rows, :], wb_ref[n])
            merged = term if merged is None else merged + term
        y = _dot(merged.astype(BF16), wo_ref[...])
        o_ref[rows, :] = h_ref[rows, :] + _rmsnorm(y, g_ref[...])


def _merge(h, u, oa, ob, oc, w_gate, b_gate, w_branch, w_out, post_g):
    t, d = h.shape
    rows = lambda w: pl.BlockSpec((TOKEN_TILE, w), lambda i: (i, 0))
    return pl.pallas_call(
        _merge_kernel,
        grid=(t // TOKEN_TILE,),
        in_specs=[rows(d), rows(d), rows(oa.shape[1]), rows(ob.shape[1]), rows(oc.shape[1]),
                  _resident(w_gate.shape), _resident(b_gate.shape), _resident(w_branch.shape),
                  _resident(w_out.shape), _resident((1, d))],
        out_specs=rows(d),
        out_shape=jax.ShapeDtypeStruct((t, d), F32),
        compiler_params=pltpu.CompilerParams(
            dimension_semantics=("arbitrary",), vmem_limit_bytes=48 * MIB),
        name="gated_merge",
    )(h, u, oa, ob, oc, w_gate, b_gate, w_branch, w_out, post_g)


def _rope_tables(s):
    pos = jnp.arange(s, dtype=F32)
    inv = ROPE_THETA ** (-jnp.arange(0, HEAD_DIM, 2, dtype=F32) / HEAD_DIM)
    ang = pos[:, None] * inv[None, :]
    cos = jnp.tile(jnp.cos(ang), (1, 2 * LANES // HEAD_DIM))
    sin = jnp.tile(jnp.sin(ang), (1, 2 * LANES // HEAD_DIM))
    first_half = (jnp.arange(LANES) % HEAD_DIM) < HEAD_DIM // 2
    return cos, jnp.where(first_half, -sin, 0.0), jnp.where(first_half, 0.0, sin)


def kernel(x, ffn1_pre_g, ffn1_w_gu, ffn1_w_down, ffn1_post_g, mix_pre_g, w_in, na_rpb, sw_sink, diff_lambda_q1, diff_lambda_k1, diff_lambda_q2, diff_lambda_k2, diff_subln_g, w_branch, b_gate, w_out, mix_post_g, ffn2_pre_g, ffn2_w_gu, ffn2_w_down, ffn2_post_g):
    b, s, d = x.shape
    depth = w_in.shape[0]
    assert s % (NA_ROWS_PER_STEP * GRID_W) == 0 and s % TOKEN_TILE == 0 and s >= NA_ROW_K * GRID_W
    assert (b * s) % FFN_TILE == 0
    cos, sin_lo, sin_hi = _rope_tables(s)
    row = lambda a: a.reshape(1, -1).astype(F32)
    h = x.reshape(b * s, d)
    for l in range(depth):
        h = _ffn(h, row(ffn1_pre_g[l]), ffn1_w_gu[l].astype(BF16), ffn1_w_down[l].astype(BF16),
                 row(ffn1_post_g[l]))

        w_l = w_in[l]
        qa_w, ka_w, va_w, qb_w, kb_w, vb_w, qc_w, kc_w, vc_w, gate_w = (
            w_l[:, c0:c1] for c0, c1 in zip(_SPLIT_EDGES[:-1], _SPLIT_EDGES[1:]))
        w_qk = jnp.concatenate([qa_w, ka_w, qb_w, kb_w, qc_w, kc_w], axis=1).astype(BF16)
        w_vt = jnp.concatenate([va_w, vb_w, vc_w], axis=1).T.astype(BF16)
        u, qa, ka, qb, kb, qc, kc, vat, vbt, vct = _proj(
            h, row(mix_pre_g[l]), w_qk, w_vt, cos, sin_lo, sin_hi)
        seq = lambda a: a.reshape(b, s, a.shape[-1])
        oa, ob = _local_attention(
            seq(qa), seq(ka), vat, _na_bias_table(na_rpb[l], s // GRID_W),
            seq(qb), seq(kb), vbt, sw_sink[l].reshape(1, -1).astype(F32) * LOG2E)
        lam_init = 0.8 - 0.6 * math.exp(-0.3 * l)
        oc = _diff(seq(qc), seq(kc), vct, row(diff_lambda_q1[l]), row(diff_lambda_k1[l]),
                   row(diff_lambda_q2[l]), row(diff_lambda_k2[l]),
                   diff_subln_g[l].reshape(-1, 1).astype(F32), lam_init)
        flat = lambda a: a.reshape(b * s, a.shape[-1])
        h = _merge(h, u, flat(oa), flat(ob), flat(oc), gate_w.astype(BF16), b_gate[l].astype(F32),
                   w_branch[l].astype(BF16), w_out[l].astype(BF16), row(mix_post_g[l]))

        h = _ffn(h, row(ffn2_pre_g[l]), ffn2_w_gu[l].astype(BF16), ffn2_w_down[l].astype(BF16),
                 row(ffn2_post_g[l]))
    return h.reshape(b, s, d)
```

```python
import functools
import math

import jax
import jax.numpy as jnp
import numpy as np
from jax import lax
from jax.experimental import pallas as pl
from jax.experimental.pallas import tpu as pltpu

F32 = jnp.float32
BF16 = jnp.bfloat16

HEAD_DIM = 64
GRID_W = 64
NA_HEADS = 8
NA_ROW_K = 8
NA_COL_K = 16
SW_Q_HEADS = 8
SW_KV_HEADS = 2
SW_WINDOW = 128
DIFF_HEADS = 4
N_BRANCH = 3
ROPE_THETA = 10000.0
EPS = 1e-6
LOG2E = math.log2(math.e)
NEG = -1e30
LANES = 128
SUBLANES = 8
MIB = 1024 * 1024

A_W = NA_HEADS * HEAD_DIM
B_QW = SW_Q_HEADS * HEAD_DIM
B_KVW = SW_KV_HEADS * HEAD_DIM
C_W = 2 * DIFF_HEADS * HEAD_DIM
_SPLIT_EDGES = (0, A_W, 2 * A_W, 3 * A_W, 3 * A_W + B_QW, 3 * A_W + B_QW + B_KVW,
                3 * A_W + B_QW + 2 * B_KVW, 3 * A_W + B_QW + 2 * B_KVW + C_W,
                3 * A_W + B_QW + 2 * B_KVW + 2 * C_W, 3 * A_W + B_QW + 2 * B_KVW + 3 * C_W, None)

TOKEN_TILE = 1024
FFN_TILE = 1024
MERGE_FFN_TILE = 512
SUB_ROWS = 256
NA_ROWS_PER_STEP = 4
SW_Q_TILE = 256
DIFF_Q_TILE = 256
DIFF_KV_TILE = 512
DIFF_Q_SUBTILES = 4
DIFF_PIECE_ROWS = 16
DIFF_WARM_ROWS = 128
DIFF_SUM_LIMIT = 2.0 ** 40
HEAD_PIPELINE_DEPTH = 4


def _rmsnorm(x, g):
    return x * lax.rsqrt(jnp.mean(x * x, axis=-1, keepdims=True) + EPS) * g


def _dot(a, b):
    return jnp.dot(a, b, preferred_element_type=F32)


def _dot_nt(a, b):
    return lax.dot_general(a, b, (((1,), (1,)), ((), ())), preferred_element_type=F32)


def _resident(shape):
    return pl.BlockSpec(shape, lambda *_: (0,) * len(shape), pipeline_mode=pl.Buffered(1))


def _low_half_mask():
    return lax.broadcasted_iota(jnp.int32, (1, LANES), 1) < HEAD_DIM


def _fold8(x, op):
    return op(x.reshape(x.shape[0] // SUBLANES, SUBLANES, x.shape[1]), axis=0)


def _col_max(x8):
    return jnp.max(x8, axis=0, keepdims=True)


def _head_pipeline(heads, scores, attend, depth=HEAD_PIPELINE_DEPTH):
    heads = list(heads)
    ready = [scores(h) for h in heads[:depth]]
    out = []
    for idx, head in enumerate(heads):
        if idx + depth < len(heads):
            ready.append(scores(heads[idx + depth]))
        out.append(attend(head, ready.pop(0)))
    return out


def _ffn_rows(x, pre_g_ref, wgu_ref, wdown_ref, post_g_ref, f_chunks, d_ff):
    xn = _rmsnorm(x, pre_g_ref[...]).astype(BF16)
    acc = None
    for c0, c1 in f_chunks:
        gate = _dot(xn, wgu_ref[:, c0:c1])
        up = _dot(xn, wgu_ref[:, d_ff + c0:d_ff + c1])
        act = (gate * jax.nn.sigmoid(gate) * up).astype(BF16)
        part = _dot(act, wdown_ref[c0:c1, :])
        acc = part if acc is None else acc + part
    return x + 0.5 * _rmsnorm(acc, post_g_ref[...])


def _ffn_kernel(x_ref, pre_g_ref, wgu_ref, wdown_ref, post_g_ref, o_ref, *, f_chunks, d_ff):
    for r0 in range(0, x_ref.shape[0], SUB_ROWS):
        rows = slice(r0, r0 + SUB_ROWS)
        o_ref[rows, :] = _ffn_rows(x_ref[rows, :], pre_g_ref, wgu_ref, wdown_ref, post_g_ref,
                                   f_chunks, d_ff)


def _f_chunks(d_ff, step=1024):
    return tuple((c, min(c + step, d_ff)) for c in range(0, d_ff, step))


def _ffn(h, pre_g, w_gu, w_down, post_g):
    t, d = h.shape
    d_ff = w_down.shape[0]
    f_chunks = _f_chunks(d_ff)
    row = pl.BlockSpec((FFN_TILE, d), lambda i: (i, 0))
    return pl.pallas_call(
        functools.partial(_ffn_kernel, f_chunks=f_chunks, d_ff=d_ff),
        grid=(t // FFN_TILE,),
        in_specs=[row, _resident((1, d)), _resident(w_gu.shape), _resident(w_down.shape),
                  _resident((1, d))],
        out_specs=row,
        out_shape=jax.ShapeDtypeStruct((t, d), F32),
        compiler_params=pltpu.CompilerParams(
            dimension_semantics=("arbitrary",), vmem_limit_bytes=48 * MIB),
        name="ffn",
    )(h, pre_g, w_gu, w_down, post_g)


def _rope(t, cos, sin_lo, sin_hi):
    out = []
    for c in range(t.shape[1] // LANES):
        tc = t[:, c * LANES:(c + 1) * LANES]
        fwd = pltpu.roll(tc, LANES - HEAD_DIM // 2, axis=1)
        back = pltpu.roll(tc, HEAD_DIM // 2, axis=1)
        out.append(tc * cos + fwd * sin_lo + back * sin_hi)
    return jnp.concatenate(out, axis=1)


def _proj_kernel(h_ref, g_ref, wqk_ref, wvt_ref, cos_ref, sin_lo_ref, sin_hi_ref,
                 u_ref, qa_ref, ka_ref, qb_ref, kb_ref, qc_ref, kc_ref, vat_ref, vbt_ref, vct_ref):
    scale = HEAD_DIM ** -0.5 * LOG2E
    for r0 in range(0, h_ref.shape[0], SUB_ROWS):
        rows = slice(r0, r0 + SUB_ROWS)
        un = _rmsnorm(h_ref[rows, :], g_ref[...]).astype(BF16)
        u_ref[rows, :] = un
        cos, sin_lo, sin_hi = cos_ref[rows, :], sin_lo_ref[rows, :], sin_hi_ref[rows, :]
        col = 0
        for out_ref, rotary, scaled in ((qa_ref, False, True), (ka_ref, False, False),
                                        (qb_ref, True, True), (kb_ref, True, False),
                                        (qc_ref, True, True), (kc_ref, True, False)):
            width = out_ref.shape[1]
            t = _dot(un, wqk_ref[:, col:col + width])
            col += width
            if rotary:
                t = _rope(t, cos, sin_lo, sin_hi)
            out_ref[rows, :] = (t * scale if scaled else t).astype(BF16)
        row = 0
        for vt_ref in (vat_ref, vbt_ref, vct_ref):
            width = vt_ref.shape[0]
            vt_ref[:, rows] = _dot_nt(wvt_ref[row:row + width, :], un).astype(BF16)
            row += width


def _proj(h, g, w_qk, w_vt, cos, sin_lo, sin_hi):
    t, d = h.shape
    s = cos.shape[0]
    tiles_per_seq = s // TOKEN_TILE
    widths = (d, A_W, A_W, B_QW, B_KVW, C_W, C_W)
    v_widths = (A_W, B_KVW, C_W)
    rows = lambda w: pl.BlockSpec((TOKEN_TILE, w), lambda i: (i, 0))
    chans = lambda w: pl.BlockSpec((None, w, TOKEN_TILE),
                                   lambda i: (i // tiles_per_seq, 0, i % tiles_per_seq))
    table = pl.BlockSpec((TOKEN_TILE, LANES), lambda i: (i % tiles_per_seq, 0))
    return pl.pallas_call(
        _proj_kernel,
        grid=(t // TOKEN_TILE,),
        in_specs=[rows(d), _resident((1, d)), _resident(w_qk.shape), _resident(w_vt.shape),
                  table, table, table],
        out_specs=[rows(w) for w in widths] + [chans(w) for w in v_widths],
        out_shape=([jax.ShapeDtypeStruct((t, w), BF16) for w in widths]
                   + [jax.ShapeDtypeStruct((t // s, w, s), BF16) for w in v_widths]),
        compiler_params=pltpu.CompilerParams(
            dimension_semantics=("arbitrary",), vmem_limit_bytes=48 * MIB),
        name="proj",
    )(h, g, w_qk, w_vt, cos, sin_lo, sin_hi)


def _na_stage(q_ref, kp_ref, kc_ref, kn_ref, vtp_ref, vtc_ref, vtn_ref, bias_ref, o_ref):
    blk = q_ref.shape[0]
    k_refs = (kp_ref, kc_ref, kn_ref)
    vt_refs = (vtp_ref, vtc_ref, vtn_ref)
    low = _low_half_mask()
    top = lax.broadcasted_iota(jnp.int32, (LANES, 1), 0) < HEAD_DIM
    lanes = lambda head: slice(head // 2 * LANES, (head // 2 + 1) * LANES)

    def scores(head):
        qc = q_ref[:, lanes(head)]
        qz = jnp.where(low if head % 2 == 0 else ~low, qc, jnp.zeros_like(qc))
        return [_dot_nt(k_refs[w][:, lanes(head)], qz) + bias_ref[head, w * blk:(w + 1) * blk, :]
                for w in range(3)]

    ones_rows = jnp.ones((2 * SUBLANES, blk), BF16)
    vt_ones = {}

    def values_and_ones(w, chunk):
        if (w, chunk) not in vt_ones:
            vt_ones[w, chunk] = jnp.concatenate(
                [vt_refs[w][chunk * LANES:(chunk + 1) * LANES, :], ones_rows], axis=0)
        return vt_ones[w, chunk]

    def attend(head, s):
        m = jnp.max(jnp.maximum(jnp.maximum(s[0], s[1]), s[2]), axis=0, keepdims=True)
        o = None
        for w in range(3):
            pv = _dot(values_and_ones(w, head // 2), jnp.exp2(s[w] - m).astype(BF16))
            o = pv if o is None else o + pv
        return o[:LANES] / o[LANES:LANES + 1]

    def finish(res):
        for c in range(A_W // LANES):
            o_ref[:, c * LANES:(c + 1) * LANES] = (
                jnp.where(top, res[2 * c], res[2 * c + 1]).T.astype(o_ref.dtype))

    return scores, attend, finish


def _na_bias_table(rpb, grid_rows):
    n = NA_ROWS_PER_STEP
    c = np.arange(GRID_W)
    dc = np.clip(c[:, None] - c[None, :] + (NA_COL_K - 1), 0, 2 * NA_COL_K - 2)
    col_start = np.clip(c - NA_COL_K // 2, 0, GRID_W - NA_COL_K)
    col_ok = (c[:, None] >= col_start[None, :]) & (c[:, None] < col_start[None, :] + NA_COL_K)
    one_hot = jnp.asarray(dc[None] == np.arange(2 * NA_COL_K - 1)[:, None, None], F32)
    expanded = jnp.einsum("hdj,jkq->hdkq", rpb.astype(F32), one_hot,
                          precision=lax.Precision.HIGHEST)
    by_offset = jnp.where(col_ok, expanded * LOG2E, NEG)
    masked = jnp.full_like(by_offset[:, 0], NEG)
    tables = []
    for first_row in (0, n, grid_rows - n):
        key_rows = []
        for kr in range(3 * n):
            blocks = []
            for qr in range(n):
                q_abs, k_abs = first_row + qr, first_row - n + kr
                row_start = min(max(q_abs - NA_ROW_K // 2, 0), grid_rows - NA_ROW_K)
                in_window = row_start <= k_abs < row_start + NA_ROW_K
                blocks.append(by_offset[:, k_abs - q_abs + NA_ROW_K - 1] if in_window else masked)
            key_rows.append(jnp.concatenate(blocks, axis=-1))
        tables.append(jnp.concatenate(key_rows, axis=1))
    return jnp.stack(tables)


def _na_specs(s, w, bias_shape):
    blk = NA_ROWS_PER_STEP * GRID_W
    n_blk = s // blk
    clamp = lambda j: jnp.clip(j, 0, n_blk - 1)
    tok = lambda shift: pl.BlockSpec((None, blk, w), lambda bi, j: (bi, clamp(j + shift), 0))
    chan = lambda shift: pl.BlockSpec((None, w, blk), lambda bi, j: (bi, 0, clamp(j + shift)))
    table = pl.BlockSpec(
        (None,) + bias_shape[1:],
        lambda bi, j: (jnp.where(j == 0, 0, jnp.where(j == n_blk - 1, 2, 1)), 0, 0, 0))
    return [tok(0), tok(-1), tok(0), tok(1), chan(-1), chan(0), chan(1), table], tok(0)


def _sw_stage(sink_ref, q_ref, kp_ref, kc_ref, kn_ref, vp_ref, vc_ref, vn_ref, o_ref):
    i = pl.program_id(1)
    tq = q_ref.shape[0]
    win = SW_WINDOW
    k = jnp.concatenate([kp_ref[...], kc_ref[...], kn_ref[...]], axis=0)
    vt = jnp.concatenate([vp_ref[...], vc_ref[...], vn_ref[...]], axis=1)
    vt_ones = jnp.concatenate([vt, jnp.ones((2 * SUBLANES, vt.shape[1]), BF16)], axis=0)
    n_kblk = k.shape[0] // win
    key_row = lax.broadcasted_iota(jnp.int32, (win, win), 0)
    query_col = lax.broadcasted_iota(jnp.int32, (win, win), 1)
    band = {0: jnp.where(key_row >= query_col, 0.0, NEG), 1: None,
            2: jnp.where(key_row <= query_col, 0.0, NEG)}
    outside = {0: jnp.where(i == 0, NEG, 0.0),
               n_kblk - 1: jnp.where(i == pl.num_programs(1) - 1, NEG, 0.0)}
    masks = {}
    for c in range(tq // win):
        for d in range(3):
            a = c + d
            mask = band[d]
            if a in outside:
                mask = outside[a] if mask is None else mask + outside[a]
            masks[a, c] = mask
    low = _low_half_mask()
    group = SW_Q_HEADS // SW_KV_HEADS

    def scores(head):
        kv_head = head // group
        qc = q_ref[:, head // 2 * LANES:(head // 2 + 1) * LANES]
        src = qc if head % 2 == kv_head else pltpu.roll(qc, HEAD_DIM, axis=1)
        qz = jnp.where(low if kv_head == 0 else ~low, src, jnp.zeros_like(src))
        return _dot_nt(k, qz)

    def attend(head, s):
        kv_head = head // group
        sink = sink_ref[0, head]
        p_cols, l_cols = [], []
        for c in range(tq // win):
            blocks = []
            for a in range(c, c + 3):
                blk = s[a * win:(a + 1) * win, c * win:(c + 1) * win]
                blocks.append(blk if masks[a, c] is None else blk + masks[a, c])
            m = jnp.maximum(jnp.max(jnp.maximum(jnp.maximum(blocks[0], blocks[1]), blocks[2]),
                                    axis=0, keepdims=True), sink)
            packed = [jnp.exp2(blk - m).astype(BF16) for blk in blocks]
            l_cols.append(jnp.exp2(sink - m))
            empty = jnp.zeros((win, win), BF16)
            p_cols.append(jnp.concatenate(
                [empty] * c + packed + [empty] * (n_kblk - c - 3), axis=0))
        p_all = jnp.concatenate(p_cols, axis=1)
        o = _dot(vt_ones, p_all)
        o = o[:LANES] / (o[LANES:LANES + 1] + jnp.concatenate(l_cols, axis=1))
        return o[kv_head * HEAD_DIM:(kv_head + 1) * HEAD_DIM]

    def finish(res):
        for c in range(B_QW // LANES):
            o_ref[:, c * LANES:(c + 1) * LANES] = (
                jnp.concatenate(res[2 * c:2 * c + 2], axis=0).T.astype(o_ref.dtype))

    return scores, attend, finish


def _sw_specs(s, w, kvw):
    tq = SW_Q_TILE
    per = tq // SW_WINDOW
    n_blk = s // SW_WINDOW
    prev = lambda i: jnp.maximum(per * i - 1, 0)
    nxt = lambda i: jnp.minimum(per * (i + 1), n_blk - 1)
    q_spec = pl.BlockSpec((None, tq, w), lambda bi, i: (bi, i, 0))
    k_specs = [pl.BlockSpec((None, SW_WINDOW, kvw), lambda bi, i: (bi, prev(i), 0)),
               pl.BlockSpec((None, tq, kvw), lambda bi, i: (bi, i, 0)),
               pl.BlockSpec((None, SW_WINDOW, kvw), lambda bi, i: (bi, nxt(i), 0))]
    vt_specs = [pl.BlockSpec((None, kvw, SW_WINDOW), lambda bi, i: (bi, 0, prev(i))),
                pl.BlockSpec((None, kvw, tq), lambda bi, i: (bi, 0, i)),
                pl.BlockSpec((None, kvw, SW_WINDOW), lambda bi, i: (bi, 0, nxt(i)))]
    sink_spec = pl.BlockSpec(memory_space=pltpu.SMEM)
    return [sink_spec, q_spec] + k_specs + vt_specs, q_spec


def _local_kernel(*refs):
    n_na = 8
    na = _na_stage(*refs[:n_na], refs[-2])
    sw = _sw_stage(*refs[n_na:-2], refs[-1])
    jobs = [(stage, head) for head in range(NA_HEADS) for stage in (na, sw)]
    res = _head_pipeline(jobs, lambda job: job[0][0](job[1]),
                         lambda job, s: job[0][1](job[1], s))
    na[2](res[0::2])
    sw[2](res[1::2])


def _local_attention(qa, ka, vat, bias, qb, kb, vbt, sink):
    b, s, w = qa.shape
    assert NA_ROWS_PER_STEP * GRID_W == SW_Q_TILE and NA_HEADS == SW_Q_HEADS
    na_in, na_out = _na_specs(s, w, bias.shape)
    sw_in, sw_out = _sw_specs(s, w, kb.shape[-1])
    out = jax.ShapeDtypeStruct((b, s, w), BF16)
    return pl.pallas_call(
        _local_kernel,
        grid=(b, s // SW_Q_TILE),
        in_specs=na_in + sw_in,
        out_specs=[na_out, sw_out],
        out_shape=[out, out],
        compiler_params=pltpu.CompilerParams(
            dimension_semantics=("arbitrary", "arbitrary"), vmem_limit_bytes=40 * MIB),
        name="local_attn",
    )(qa, ka, ka, ka, vat, vat, vat, bias, sink, qb, kb, kb, kb, vbt, vbt, vbt)


def _diff_kernel(q_ref, k_ref, vt_ref, lq1_ref, lk1_ref, lq2_ref, lk2_ref, g_ref, o_ref,
                 m_ref, *scratch, lam_init):
    tq = DIFF_Q_TILE
    n_sub = q_ref.shape[0] // tq
    per = len(scratch) // n_sub
    subs = [scratch[i * per:(i + 1) * per] for i in range(n_sub)]
    kv_tile = subs[0][2].shape[0]
    n_kv = k_ref.shape[0] // kv_tile
    cols = 2 * tq
    low = _low_half_mask()

    lam = (jnp.exp(jnp.sum(lq1_ref[...] * lk1_ref[...]))
           - jnp.exp(jnp.sum(lq2_ref[...] * lk2_ref[...])) + lam_init)

    def finalize(sub):
        l_ref, acc_ref = subs[sub][:2]
        o = acc_ref[...] / l_ref[...]
        o = o[:, :tq] - lam * o[:, tq:]
        o = o * lax.rsqrt(jnp.mean(o * o, axis=0, keepdims=True) + EPS) * g_ref[...]
        o_ref[sub * tq:(sub + 1) * tq, :] = (o * (1.0 - lam_init)).T.astype(o_ref.dtype)

    def fast_sweep(sub):
        l_ref, acc_ref, p0_ref, p1_ref, qz_ref = subs[sub]
        p_bufs = (p0_ref, p1_ref)
        q = q_ref[sub * tq:(sub + 1) * tq, :]
        zero = jnp.zeros_like(q)
        qz_ref[0:tq, :] = jnp.where(low, q, zero)
        qz_ref[tq:, :] = jnp.where(low, zero, q)
        ref = None
        unsafe = jnp.zeros((1, cols), F32)
        l8 = None
        beta = None
        pending = None

        def value_product(tile, scale):
            pv = _dot(vt_ref[:, tile * kv_tile:(tile + 1) * kv_tile], p_bufs[tile % 2][...])
            acc_ref[...] = pv if scale is None else scale * acc_ref[...] + pv

        for t in range(n_kv):
            tile_sum8 = None
            s = _dot_nt(k_ref[t * kv_tile:(t + 1) * kv_tile, :], qz_ref[...])
            if ref is None:
                ref = _col_max(_fold8(s[0:DIFF_WARM_ROWS], jnp.max))
            if pending is not None:
                value_product(*pending)
            for r in range(0, kv_tile, DIFF_PIECE_ROWS):
                p = jnp.exp2(s[r:r + DIFF_PIECE_ROWS] - ref)
                ps = _fold8(p, jnp.sum)
                tile_sum8 = ps if tile_sum8 is None else tile_sum8 + ps
                p_bufs[t % 2][r:r + DIFF_PIECE_ROWS, :] = p.astype(BF16)
            pending = (t, beta)
            l8 = tile_sum8 if beta is None else beta * l8 + tile_sum8
            grow = jnp.maximum(jnp.sum(tile_sum8, axis=0, keepdims=True), 1.0)
            unsafe = jnp.maximum(unsafe, jnp.where(grow > DIFF_SUM_LIMIT, 1.0, 0.0))
            beta = 1.0 / grow
            ref = ref + jnp.log2(grow)
        value_product(*pending)
        l_fast = jnp.sum(l8, axis=0, keepdims=True)
        l_ref[...] = l_fast
        finite = jnp.isfinite(jnp.sum(acc_ref[...], axis=0, keepdims=True) + l_fast)
        return jnp.maximum(unsafe, jnp.where(finite, 0.0, 1.0))

    def exact_sweep(sub):
        l_ref, acc_ref, _, _, qz_ref = subs[sub]
        m_ref[...] = jnp.full(m_ref.shape, -jnp.inf, F32)
        l_ref[...] = jnp.zeros(l_ref.shape, F32)
        acc_ref[...] = jnp.zeros(acc_ref.shape, F32)

        def tile_step(j, carry):
            rows = pl.ds(pl.multiple_of(j * kv_tile, kv_tile), kv_tile)
            s = _dot_nt(k_ref[rows, :], qz_ref[...])
            m_old = m_ref[...]
            m_new = jnp.maximum(m_old, jnp.max(s, axis=0, keepdims=True))
            alpha = jnp.exp2(m_old - m_new)
            p = jnp.exp2(s - m_new)
            m_ref[...] = m_new
            l_ref[...] = alpha * l_ref[...] + jnp.sum(p, axis=0, keepdims=True)
            acc_ref[...] = alpha * acc_ref[...] + _dot(vt_ref[:, rows], p.astype(BF16))
            return carry

        lax.fori_loop(0, n_kv, tile_step, 0)

    unsafe = None
    for sub in range(n_sub):
        flag = fast_sweep(sub)
        unsafe = flag if unsafe is None else jnp.maximum(unsafe, flag)
        finalize(sub)

    @pl.when(jnp.max(unsafe) > 0.0)
    def _():
        for sub in range(n_sub):
            exact_sweep(sub)
            finalize(sub)


def _diff(q, k, vt, lq1, lk1, lq2, lk2, subln_g, lam_init):
    b, s, w = q.shape
    tq = DIFF_Q_TILE
    dv = 2 * HEAD_DIM
    heads = w // LANES
    kv_tile = min(DIFF_KV_TILE, s)
    n_sub = min(DIFF_Q_SUBTILES, s // tq)
    q_spec = pl.BlockSpec((None, n_sub * tq, LANES), lambda bi, h, i: (bi, i, h))
    k_spec = pl.BlockSpec((None, s, LANES), lambda bi, h, i: (bi, 0, h))
    vt_spec = pl.BlockSpec((None, dv, s), lambda bi, h, i: (bi, h, 0))
    vec = lambda n: pl.BlockSpec((1, n), lambda bi, h, i: (0, 0))
    sub_scratch = [pltpu.VMEM((1, 2 * tq), F32), pltpu.VMEM((dv, 2 * tq), F32),
                   pltpu.VMEM((kv_tile, 2 * tq), BF16), pltpu.VMEM((kv_tile, 2 * tq), BF16),
                   pltpu.VMEM((2 * tq, LANES), BF16)]
    return pl.pallas_call(
        functools.partial(_diff_kernel, lam_init=lam_init),
        grid=(b, heads, s // (n_sub * tq)),
        in_specs=[q_spec, k_spec, vt_spec, vec(HEAD_DIM), vec(HEAD_DIM), vec(HEAD_DIM),
                  vec(HEAD_DIM), pl.BlockSpec((dv, 1), lambda bi, h, i: (0, 0))],
        out_specs=q_spec,
        out_shape=jax.ShapeDtypeStruct((b, s, w), BF16),
        scratch_shapes=[pltpu.VMEM((1, 2 * tq), F32)] + sub_scratch * n_sub,
        compiler_params=pltpu.CompilerParams(
            dimension_semantics=("arbitrary", "arbitrary", "arbitrary"),
            vmem_limit_bytes=32 * MIB),
        name="differential_attn",
    )(q, k, vt, lq1, lk1, lq2, lk2, subln_g)


def _merge_ffn_kernel(h_ref, u_ref, oa_ref, ob_ref, oc_ref, wg_ref, bg_ref, wb_ref, wo_ref, g_ref,
                      pre_g_ref, wgu_ref, wdown_ref, post_g_ref, o_ref, *, f_chunks, d_ff):
    d = h_ref.shape[1]
    for r0 in range(0, h_ref.shape[0], SUB_ROWS):
        rows = slice(r0, r0 + SUB_ROWS)
        u = u_ref[rows, :]
        merged = None
        for n, br_ref in enumerate((oa_ref, ob_ref, oc_ref)):
            gate = jax.nn.sigmoid(_dot(u, wg_ref[:, n * d:(n + 1) * d]) + bg_ref[n:n + 1, :])
            term = gate * _dot(br_ref[rows, :], wb_ref[n])
            merged = term if merged is None else merged + term
        y = _dot(merged.astype(BF16), wo_ref[...])
        mixed = h_ref[rows, :] + _rmsnorm(y, g_ref[...])
        o_ref[rows, :] = _ffn_rows(mixed, pre_g_ref, wgu_ref, wdown_ref, post_g_ref,
                                   f_chunks, d_ff)


def _merge_ffn(h, u, oa, ob, oc, w_gate, b_gate, w_branch, w_out, post_g,
               ffn_pre_g, w_gu, w_down, ffn_post_g):
    t, d = h.shape
    d_ff = w_down.shape[0]
    rows = lambda w: pl.BlockSpec((MERGE_FFN_TILE, w), lambda i: (i, 0))
    return pl.pallas_call(
        functools.partial(_merge_ffn_kernel, f_chunks=_f_chunks(d_ff), d_ff=d_ff),
        grid=(t // MERGE_FFN_TILE,),
        in_specs=[rows(d), rows(d), rows(oa.shape[1]), rows(ob.shape[1]), rows(oc.shape[1]),
                  _resident(w_gate.shape), _resident(b_gate.shape), _resident(w_branch.shape),
                  _resident(w_out.shape), _resident((1, d)),
                  _resident((1, d)), _resident(w_gu.shape), _resident(w_down.shape),
                  _resident((1, d))],
        out_specs=rows(d),
        out_shape=jax.ShapeDtypeStruct((t, d), F32),
        compiler_params=pltpu.CompilerParams(
            dimension_semantics=("arbitrary",), vmem_limit_bytes=56 * MIB),
        name="merge_ffn",
    )(h, u, oa, ob, oc, w_gate, b_gate, w_branch, w_out, post_g,
      ffn_pre_g, w_gu, w_down, ffn_post_g)


def _rope_tables(s):
    pos = jnp.arange(s, dtype=F32)
    inv = ROPE_THETA ** (-jnp.arange(0, HEAD_DIM, 2, dtype=F32) / HEAD_DIM)
    ang = pos[:, None] * inv[None, :]
    cos = jnp.tile(jnp.cos(ang), (1, 2 * LANES // HEAD_DIM))
    sin = jnp.tile(jnp.sin(ang), (1, 2 * LANES // HEAD_DIM))
    first_half = (jnp.arange(LANES) % HEAD_DIM) < HEAD_DIM // 2
    return cos, jnp.where(first_half, -sin, 0.0), jnp.where(first_half, 0.0, sin)


def kernel(x, ffn1_pre_g, ffn1_w_gu, ffn1_w_down, ffn1_post_g, mix_pre_g, w_in, na_rpb, sw_sink, diff_lambda_q1, diff_lambda_k1, diff_lambda_q2, diff_lambda_k2, diff_subln_g, w_branch, b_gate, w_out, mix_post_g, ffn2_pre_g, ffn2_w_gu, ffn2_w_down, ffn2_post_g):
    b, s, d = x.shape
    depth = w_in.shape[0]
    assert s % (NA_ROWS_PER_STEP * GRID_W) == 0 and s % TOKEN_TILE == 0 and s >= NA_ROW_K * GRID_W
    assert (b * s) % FFN_TILE == 0
    cos, sin_lo, sin_hi = _rope_tables(s)
    row = lambda a: a.reshape(1, -1).astype(F32)
    h = x.reshape(b * s, d)
    for l in range(depth):
        h = _ffn(h, row(ffn1_pre_g[l]), ffn1_w_gu[l].astype(BF16), ffn1_w_down[l].astype(BF16),
                 row(ffn1_post_g[l]))

        w_l = w_in[l]
        qa_w, ka_w, va_w, qb_w, kb_w, vb_w, qc_w, kc_w, vc_w, gate_w = (
            w_l[:, c0:c1] for c0, c1 in zip(_SPLIT_EDGES[:-1], _SPLIT_EDGES[1:]))
        w_qk = jnp.concatenate([qa_w, ka_w, qb_w, kb_w, qc_w, kc_w], axis=1).astype(BF16)
        w_vt = jnp.concatenate([va_w, vb_w, vc_w], axis=1).T.astype(BF16)
        u, qa, ka, qb, kb, qc, kc, vat, vbt, vct = _proj(
            h, row(mix_pre_g[l]), w_qk, w_vt, cos, sin_lo, sin_hi)
        seq = lambda a: a.reshape(b, s, a.shape[-1])
        oa, ob = _local_attention(
            seq(qa), seq(ka), vat, _na_bias_table(na_rpb[l], s // GRID_W),
            seq(qb), seq(kb), vbt, sw_sink[l].reshape(1, -1).astype(F32) * LOG2E)
        lam_init = 0.8 - 0.6 * math.exp(-0.3 * l)
        oc = _diff(seq(qc), seq(kc), vct, row(diff_lambda_q1[l]), row(diff_lambda_k1[l]),
                   row(diff_lambda_q2[l]), row(diff_lambda_k2[l]),
                   diff_subln_g[l].reshape(-1, 1).astype(F32), lam_init)
        flat = lambda a: a.reshape(b * s, a.shape[-1])
        h = _merge_ffn(h, u, flat(oa), flat(ob), flat(oc), gate_w.astype(BF16),
                       b_gate[l].astype(F32), w_branch[l].astype(BF16), w_out[l].astype(BF16),
                       row(mix_post_g[l]), row(ffn2_pre_g[l]), ffn2_w_gu[l].astype(BF16),
                       ffn2_w_down[l].astype(BF16), row(ffn2_post_g[l]))
    return h.reshape(b, s, d)
```
